```python
import math
import jax, jax.numpy as jnp
from jax import lax
import numpy as np

D_MODEL = 1024
BATCH = 4
SEQ = 8192
DEPTH = 2
DEC_BATCH = 4
DEC_SEQ = 4096
PAST_LEN = 128

GRID_W = 64
N_MIXERS = 2
N_A_LAYERS = (DEPTH + 1) // 2
N_B_LAYERS = DEPTH // 2
S5_GROUP = 16
S5_GROUPS = D_MODEL // S5_GROUP
S5_STATE = 64
DT_MIN = 0.001
DT_MAX = 0.1
NA_HEADS = 16
NA_HEAD_DIM = D_MODEL // NA_HEADS
NA_WIN_ROWS = 8
NA_WIN_COLS = 16
NA_QCOLS = 16
NA_KCOLS = 32
MEM_HEADS = 4
MEM_HEAD_DIM = D_MODEL // MEM_HEADS
MEM_LEN = 256
N_EXPERTS = 16
EXPERT_FF = 2048
EC_CAPACITY_FACTOR = 2
DN_ALPHA = (2.0 * DEPTH) ** 0.25
DN_BETA = (8.0 * DEPTH) ** -0.25
LN_EPS = 1e-5

kernel_name = "hybrid_s5_natten_ec_moe_encoder"


def layer_norm(x, g, b):
    xf = x.astype(jnp.float32)
    mu = jnp.mean(xf, axis=-1, keepdims=True)
    var = jnp.mean(jnp.square(xf - mu), axis=-1, keepdims=True)
    y = (xf - mu) * lax.rsqrt(var + LN_EPS) * g.astype(jnp.float32) + b.astype(jnp.float32)
    return y.astype(x.dtype)


def _ssm_combine(left, right):
    a_l, b_l = left
    a_r, b_r = right
    return a_r * a_l, a_r * b_l + b_r


def s5_mixer(x, w_in, lam_re, lam_im, log_dt, b_re, b_im, c_re, c_im, d_skip, w_glu):
    bsz, L, _ = x.shape
    f32 = jnp.float32
    u = (x @ w_in).astype(f32)
    ug = u.reshape(bsz, L, S5_GROUPS, S5_GROUP)
    y = d_skip.astype(f32) * u
    for direction, reverse in ((0, False), (1, True)):
        dt = jnp.exp(log_dt[direction].astype(f32))[:, None]
        lam = lax.complex(lam_re[direction].astype(f32), lam_im[direction].astype(f32))
        lam_bar = jnp.exp(lam * dt)
        b_bar = ((lam_bar - 1.0) / lam)[..., None] * lax.complex(
            b_re[direction].astype(f32), b_im[direction].astype(f32))
        bu = lax.complex(jnp.einsum('blgh,gph->blgp', ug, jnp.real(b_bar)),
                         jnp.einsum('blgh,gph->blgp', ug, jnp.imag(b_bar)))
        a = jnp.broadcast_to(lam_bar, (1, L) + lam_bar.shape)
        _, h = lax.associative_scan(_ssm_combine, (a, bu), axis=1, reverse=reverse)
        y_dir = (jnp.einsum('gcp,blgp->blgc', c_re[direction].astype(f32), jnp.real(h))
                 - jnp.einsum('gcp,blgp->blgc', c_im[direction].astype(f32), jnp.imag(h)))
        y = y + y_dir.reshape(bsz, L, D_MODEL)
    g = jax.nn.gelu(y).astype(x.dtype)
    z = g @ w_glu
    return z[..., :D_MODEL] * jax.nn.sigmoid(z[..., D_MODEL:])


def na_mixer(x, w_qkv, rpb, w_o):
    bsz, L, _ = x.shape
    rows = L // GRID_W
    kr = min(NA_WIN_ROWS, rows)
    n_cb = GRID_W // NA_QCOLS
    qkv = (x @ w_qkv).reshape(bsz, rows, GRID_W, 3, NA_HEADS, NA_HEAD_DIM)
    q = qkv[:, :, :, 0] * (NA_HEAD_DIM ** -0.5)
    k = qkv[:, :, :, 1]
    v = qkv[:, :, :, 2]
    q_cols = jnp.arange(GRID_W).reshape(n_cb, NA_QCOLS)
    win_start = jnp.clip(q_cols - NA_WIN_COLS // 2, 0, GRID_W - NA_WIN_COLS)
    blk_start = jnp.clip(jnp.arange(n_cb) * NA_QCOLS - NA_WIN_COLS // 2, 0, GRID_W - NA_KCOLS)
    key_cols = blk_start[:, None] + jnp.arange(NA_KCOLS)[None, :]
    kc = key_cols[:, None, :]
    col_mask = (kc >= win_start[:, :, None]) & (kc < win_start[:, :, None] + NA_WIN_COLS)
    dc_idx = jnp.clip(kc - q_cols[:, :, None] + NA_WIN_COLS - 1, 0, 2 * NA_WIN_COLS - 2)
    mask = col_mask[:, :, None, :]

    def row_step(args):
        r, q_row = args
        rs = jnp.clip(r - kr // 2, 0, rows - kr)
        k_rows = lax.dynamic_slice_in_dim(k, rs, kr, axis=1)
        v_rows = lax.dynamic_slice_in_dim(v, rs, kr, axis=1)
        k_blk = k_rows[:, :, key_cols]
        v_blk = v_rows[:, :, key_cols]
        qb = q_row.reshape(bsz, n_cb, NA_QCOLS, NA_HEADS, NA_HEAD_DIM)
        s = jnp.einsum('bnqhd,brnkhd->bhnqrk', qb, k_blk).astype(jnp.float32)
        dr_idx = (rs + jnp.arange(kr) - r + NA_WIN_ROWS - 1).reshape(kr, 1, 1, 1)
        bias = rpb[:, dr_idx, dc_idx[None]].transpose(0, 2, 3, 1, 4).astype(jnp.float32)
        s = jnp.where(mask, s + bias[None], -jnp.inf)
        p = jax.nn.softmax(s.reshape(bsz, NA_HEADS, n_cb, NA_QCOLS, kr * NA_KCOLS), axis=-1)
        p = p.reshape(bsz, NA_HEADS, n_cb, NA_QCOLS, kr, NA_KCOLS).astype(x.dtype)
        o = jnp.einsum('bhnqrk,brnkhd->bnqhd', p, v_blk)
        return o.reshape(bsz, GRID_W, D_MODEL)

    q_rows = jnp.moveaxis(q, 1, 0)
    out = lax.map(row_step, (jnp.arange(rows), q_rows))
    out = jnp.moveaxis(out, 0, 1).reshape(bsz, L, D_MODEL)
    return out @ w_o


def mem_xattn(x, mem, w_q, w_kv, w_o):
    bsz, L, _ = x.shape
    m = mem.shape[1]
    q = (x @ w_q).reshape(bsz, L, MEM_HEADS, MEM_HEAD_DIM)
    kv = (mem @ w_kv).reshape(bsz, m, 2, MEM_HEADS, MEM_HEAD_DIM)
    s = jnp.einsum('blhd,bmhd->bhlm', q, kv[:, :, 0]).astype(jnp.float32) * (MEM_HEAD_DIM ** -0.5)
    p = jax.nn.softmax(s, axis=-1).astype(x.dtype)
    o = jnp.einsum('bhlm,bmhd->blhd', p, kv[:, :, 1]).reshape(bsz, L, D_MODEL)
    return o @ w_o


def ec_moe(x, w_router, w1, w3, w2):
    bsz, L, d = x.shape
    n = bsz * L
    cap = EC_CAPACITY_FACTOR * n // N_EXPERTS
    xt = x.reshape(n, d)
    aff = jax.nn.softmax((xt @ w_router).astype(jnp.float32), axis=-1)
    gate, idx = lax.top_k(aff.T, cap)
    xs = xt[idx]
    h = jax.nn.silu(jnp.einsum('ecd,edf->ecf', xs, w1)) * jnp.einsum('ecd,edf->ecf', xs, w3)
    ye = jnp.einsum('ecf,efd->ecd', h, w2) * gate[..., None].astype(x.dtype)
    out = jnp.zeros_like(xt).at[idx.reshape(-1)].add(ye.reshape(-1, d))
    return out.reshape(bsz, L, d)


def trunk(x, mem, a_w_in, a_lam_re, a_lam_im, a_log_dt, a_b_re, a_b_im, a_c_re, a_c_im, a_d, a_w_glu,
          b_w_qkv, b_rpb, b_w_o, m_w_q, m_w_kv, m_w_o, e_w_router, e_w1, e_w3, e_w2, ln_g, ln_b):
    for i in range(DEPTH):
        j = i // N_MIXERS
        if i % N_MIXERS == 0:
            h = s5_mixer(x, a_w_in[j], a_lam_re[j], a_lam_im[j], a_log_dt[j], a_b_re[j], a_b_im[j],
                         a_c_re[j], a_c_im[j], a_d[j], a_w_glu[j])
        else:
            h = na_mixer(x, b_w_qkv[j], b_rpb[j], b_w_o[j])
        x = layer_norm(DN_ALPHA * x + h, ln_g[i, 0], ln_b[i, 0])
        x = layer_norm(DN_ALPHA * x + mem_xattn(x, mem, m_w_q[i], m_w_kv[i], m_w_o[i]), ln_g[i, 1], ln_b[i, 1])
        x = layer_norm(DN_ALPHA * x + ec_moe(x, e_w_router[i], e_w1[i], e_w3[i], e_w2[i]), ln_g[i, 2], ln_b[i, 2])
    return x


def setup_inputs(seed: int = 0) -> dict:
    key = jax.random.key(seed)
    ks = jax.random.split(key, 32)
    f32 = jnp.float32
    D, G, P, CG = D_MODEL, S5_GROUPS, S5_STATE, S5_GROUP

    def nrm(k, shape, scale):
        return jax.random.normal(k, shape, f32) * scale

    n_vec = jnp.arange(P, dtype=f32)
    return {
        "x_prompt": nrm(ks[0], (BATCH, SEQ, D), 1.0),
        "x_sample": nrm(ks[1], (DEC_BATCH, DEC_SEQ, D), 1.0),
        "mem_prompt": nrm(ks[2], (BATCH, MEM_LEN, D), 1.0),
        "mem_sample": nrm(ks[3], (DEC_BATCH, MEM_LEN, D), 1.0),
        "a_w_in": nrm(ks[4], (N_A_LAYERS, D, D), D ** -0.5),
        "a_lam_re": -0.5 + nrm(ks[5], (N_A_LAYERS, 2, G, P), 0.01),
        "a_lam_im": math.pi * n_vec + nrm(ks[6], (N_A_LAYERS, 2, G, P), 0.01),
        "a_log_dt": jax.random.uniform(ks[7], (N_A_LAYERS, 2, G), f32, math.log(DT_MIN), math.log(DT_MAX)),
        "a_b_re": nrm(ks[8], (N_A_LAYERS, 2, G, P, CG), (2.0 * CG) ** -0.5),
        "a_b_im": nrm(ks[9], (N_A_LAYERS, 2, G, P, CG), (2.0 * CG) ** -0.5),
        "a_c_re": nrm(ks[10], (N_A_LAYERS, 2, G, CG, P), (2.0 * P) ** -0.5),
        "a_c_im": nrm(ks[11], (N_A_LAYERS, 2, G, CG, P), (2.0 * P) ** -0.5),
        "a_d": nrm(ks[12], (N_A_LAYERS, D), 1.0),
        "a_w_glu": nrm(ks[13], (N_A_LAYERS, D, 2 * D), D ** -0.5 * DN_BETA),
        "b_w_qkv": nrm(ks[14], (N_B_LAYERS, D, 3 * D), D ** -0.5),
        "b_rpb": nrm(ks[15], (N_B_LAYERS, NA_HEADS, 2 * NA_WIN_ROWS - 1, 2 * NA_WIN_COLS - 1), 0.1),
        "b_w_o": nrm(ks[16], (N_B_LAYERS, D, D), D ** -0.5 * DN_BETA),
        "m_w_q": nrm(ks[17], (DEPTH, D, D), D ** -0.5),
        "m_w_kv": nrm(ks[18], (DEPTH, D, 2 * D), D ** -0.5),
        "m_w_o": nrm(ks[19], (DEPTH, D, D), D ** -0.5 * DN_BETA),
        "e_w_router": nrm(ks[20], (DEPTH, D, N_EXPERTS), D ** -0.5),
        "e_w1": nrm(ks[21], (DEPTH, N_EXPERTS, D, EXPERT_FF), D ** -0.5),
        "e_w3": nrm(ks[22], (DEPTH, N_EXPERTS, D, EXPERT_FF), D ** -0.5),
        "e_w2": nrm(ks[23], (DEPTH, N_EXPERTS, EXPERT_FF, D), EXPERT_FF ** -0.5 * DN_BETA),
        "ln_g": 1.0 + nrm(ks[24], (DEPTH, 3, D), 0.02),
        "ln_b": nrm(ks[25], (DEPTH, 3, D), 0.02),
    }


def reference(x_prompt, x_sample, mem_prompt, mem_sample, a_w_in, a_lam_re, a_lam_im, a_log_dt,
              a_b_re, a_b_im, a_c_re, a_c_im, a_d, a_w_glu, b_w_qkv, b_rpb, b_w_o,
              m_w_q, m_w_kv, m_w_o, e_w_router, e_w1, e_w3, e_w2, ln_g, ln_b):
    y_prompt = trunk(x_prompt, mem_prompt, a_w_in, a_lam_re, a_lam_im, a_log_dt, a_b_re, a_b_im,
                     a_c_re, a_c_im, a_d, a_w_glu, b_w_qkv, b_rpb, b_w_o, m_w_q, m_w_kv, m_w_o,
                     e_w_router, e_w1, e_w3, e_w2, ln_g, ln_b)
    y_sample = trunk(x_sample, mem_sample, a_w_in, a_lam_re, a_lam_im, a_log_dt, a_b_re, a_b_im,
                     a_c_re, a_c_im, a_d, a_w_glu, b_w_qkv, b_rpb, b_w_o, m_w_q, m_w_kv, m_w_o,
                     e_w_router, e_w1, e_w3, e_w2, ln_g, ln_b)
    return (y_prompt, y_sample)
```

```python
import functools
import math

import jax
import jax.numpy as jnp
from jax import lax
from jax.experimental import pallas as pl
from jax.experimental.pallas import tpu as pltpu

F32 = jnp.float32
BF16 = jnp.bfloat16
I32 = jnp.int32

D_MODEL = 1024
DEPTH = 2
GRID_W = 64
S5_GROUP = 16
S5_GROUPS = D_MODEL // S5_GROUP
S5_STATE = 64
S5_CHUNK = 16
S5_GROUPS_PER_STEP = 4
NA_HEADS = 16
NA_HEAD_DIM = D_MODEL // NA_HEADS
NA_WIN_ROWS = 8
NA_WIN_COLS = 16
NA_QCOLS = 16
NA_KCOLS = 32
NA_ROW_BLOCK = 8
NA_KEY_ROWS = 16
NA_MASKED = -1e30
MEM_HEADS = 4
MEM_HEAD_DIM = D_MODEL // MEM_HEADS
N_EXPERTS = 16
EXPERT_FF = 2048
EC_CAPACITY_FACTOR = 2
ROUTE_TILE = 256
ROUTE_ALIGN = 16
ROUTE_WIN = ROUTE_TILE + ROUTE_ALIGN
ROUTE_SMALL_WIN = 128
DISPATCH_EXPERTS = 2
DISPATCH_TOKENS = 2048
FFN_TILE = 512
FF_CHUNK = 512
DN_ALPHA = (2.0 * DEPTH) ** 0.25
LN_EPS = 1e-5

ROW_TILE = 512
VMEM_LIMIT = 56 * 1024 * 1024


def _cparams(*sem):
    return pltpu.CompilerParams(dimension_semantics=sem, vmem_limit_bytes=VMEM_LIMIT)


def _layer_norm(v, g, b):
    mu = jnp.mean(v, axis=-1, keepdims=True)
    c = v - mu
    var = jnp.mean(c * c, axis=-1, keepdims=True)
    return c * lax.rsqrt(var + LN_EPS) * g + b


def _matmul_kernel(x_ref, w_ref, o_ref):
    o_ref[...] = jnp.dot(x_ref[...].astype(BF16), w_ref[...],
                         preferred_element_type=F32).astype(o_ref.dtype)


def _matmul(x, w, out_dtype, tile=ROW_TILE):
    n, k = x.shape
    m = w.shape[1]
    tile = min(tile, n)
    return pl.pallas_call(
        _matmul_kernel,
        grid=(n // tile,),
        in_specs=[pl.BlockSpec((tile, k), lambda i: (i, 0)),
                  pl.BlockSpec((k, m), lambda i: (0, 0))],
        out_specs=pl.BlockSpec((tile, m), lambda i: (i, 0)),
        out_shape=jax.ShapeDtypeStruct((n, m), out_dtype),
        compiler_params=_cparams("parallel"),
        name="matmul",
    )(x, w)


def _s5_operators(lam_re, lam_im, log_dt, b_re, b_im, c_re, c_im, d_skip):
    hp = lax.Precision.HIGHEST
    n = S5_CHUNK
    dt = jnp.exp(log_dt.astype(F32))[..., None]
    lr = lam_re.astype(F32) * dt
    li = lam_im.astype(F32) * dt
    k = jnp.arange(n + 1, dtype=F32)
    mag = jnp.exp(lr[..., None] * k)
    pw_re = mag * jnp.cos(li[..., None] * k)
    pw_im = mag * jnp.sin(li[..., None] * k)
    x = pw_re[..., 1] - 1.0
    y = pw_im[..., 1]
    a = lam_re.astype(F32)
    b = lam_im.astype(F32)
    den = a * a + b * b
    q_re = ((x * a + y * b) / den)[..., None]
    q_im = ((y * a - x * b) / den)[..., None]
    bb_re = q_re * b_re.astype(F32) - q_im * b_im.astype(F32)
    bb_im = q_re * b_im.astype(F32) + q_im * b_re.astype(F32)
    cr = c_re.astype(F32)[:, :, None]
    ci = c_im.astype(F32)[:, :, None]
    pr = jnp.moveaxis(pw_re, -1, 2)[:, :, :, None, :]
    pi = jnp.moveaxis(pw_im, -1, 2)[:, :, :, None, :]
    cp_re = cr * pr - ci * pi
    cp_im = cr * pi + ci * pr
    kern = (jnp.einsum('dgkcp,dgph->dgkch', cp_re, bb_re, precision=hp)
            - jnp.einsum('dgkcp,dgph->dgkch', cp_im, bb_im, precision=hp))
    sig = jnp.arange(n)[:, None]
    tau = jnp.arange(n)[None, :]
    lag_f = tau - sig
    lag_b = sig - tau
    kf = jnp.where((lag_f >= 0)[None, :, :, None, None], kern[0][:, jnp.clip(lag_f, 0, n)], 0.0)
    kb = jnp.where((lag_b >= 0)[None, :, :, None, None], kern[1][:, jnp.clip(lag_b, 0, n)], 0.0)
    t_op = jnp.transpose(kf + kb, (0, 1, 4, 2, 3)).reshape(S5_GROUPS, n * 16, n * 16)
    idx_f = (n - 1) - jnp.arange(n)
    idx_b = jnp.arange(n)

    def b_cols(pw_r, pw_i, idx, d):
        pr_ = pw_r[d][:, :, idx]
        pi_ = pw_i[d][:, :, idx]
        re = pr_[..., None] * bb_re[d][:, :, None, :] - pi_[..., None] * bb_im[d][:, :, None, :]
        im = pr_[..., None] * bb_im[d][:, :, None, :] + pi_[..., None] * bb_re[d][:, :, None, :]
        return (jnp.transpose(re, (0, 2, 3, 1)).reshape(S5_GROUPS, n * 16, S5_STATE),
                jnp.transpose(im, (0, 2, 3, 1)).reshape(S5_GROUPS, n * 16, S5_STATE))

    bf_re, bf_im = b_cols(pw_re, pw_im, idx_f, 0)
    bb_re_, bb_im_ = b_cols(pw_re, pw_im, idx_b, 1)
    b_op = jnp.concatenate([bf_re, bb_re_, bf_im, bb_im_], axis=-1)
    kf_idx = jnp.arange(n) + 1
    kb_idx = n - jnp.arange(n)

    def c_rows(d, idx):
        re = cp_re[d][:, idx]
        im = cp_im[d][:, idx]
        re = jnp.transpose(re, (0, 3, 1, 2)).reshape(S5_GROUPS, S5_STATE, n * 16)
        im = jnp.transpose(im, (0, 3, 1, 2)).reshape(S5_GROUPS, S5_STATE, n * 16)
        return re, -im

    cf_re, cf_im = c_rows(0, kf_idx)
    cb_re, cb_im = c_rows(1, kb_idx)
    c_op = jnp.concatenate([cf_re, cb_re, cf_im, cb_im], axis=1)
    a_re = jnp.concatenate([pw_re[0][..., n], pw_re[1][..., n]], axis=-1)
    a_im = jnp.concatenate([pw_im[0][..., n], pw_im[1][..., n]], axis=-1)
    a_op = jnp.stack([a_re, a_im], axis=1)
    a_op = jnp.broadcast_to(a_op[:, :, None, :], (S5_GROUPS, 2, 8, 128))
    d_op = jnp.tile(d_skip.astype(F32).reshape(S5_GROUPS, 1, 16), (1, 1, n))
    return t_op.astype(BF16), b_op.astype(BF16), c_op.astype(BF16), a_op, d_op


def _s5_kernel(u_ref, t_ref, b_ref, c_ref, a_ref, d_ref, o_ref, s_scr, hc_scr, *, n_tiles):
    gb = u_ref.shape[0]
    for gi in range(gb):
        s_scr[gi] = jnp.dot(u_ref[gi].astype(BF16), b_ref[gi], preferred_element_type=F32)

    row = lax.broadcasted_iota(I32, (8, 128), 0)
    lane = lax.broadcasted_iota(I32, (8, 128), 1)
    low_rows = row < 4
    fwd_lanes = lane < S5_STATE
    a_re = [a_ref[gi, 0] for gi in range(gb)]
    a_im = [a_ref[gi, 1] for gi in range(gb)]

    def cmul_add(gi, h_re, h_im, v_re, v_im):
        return (a_re[gi] * h_re - a_im[gi] * h_im + v_re,
                a_re[gi] * h_im + a_im[gi] * h_re + v_im)

    def half_steps(gi, k, c_re, c_im):
        rows = pl.ds(pl.multiple_of(k * 8, 8), 8)
        v_re = s_scr[gi, rows, 0:128]
        v_im = s_scr[gi, rows, 128:256]
        h1_re, h1_im = cmul_add(gi, c_re, c_im, v_re, v_im)
        h1s_re = pltpu.roll(h1_re, 4, 0)
        h1s_im = pltpu.roll(h1_im, 4, 0)
        h2_re, h2_im = cmul_add(gi, h1s_re, h1s_im, v_re, v_im)
        return rows, h1s_re, h1s_im, pltpu.roll(h2_re, 4, 0), pltpu.roll(h2_im, 4, 0)

    def fwd_body(k, carry):
        out = []
        for gi in range(gb):
            c_re, c_im = carry[2 * gi], carry[2 * gi + 1]
            rows, h1s_re, h1s_im, n_re, n_im = half_steps(gi, k, c_re, c_im)
            hc_scr[gi, rows, 0:128] = jnp.where(low_rows, c_re, h1s_re)
            hc_scr[gi, rows, 128:256] = jnp.where(low_rows, c_im, h1s_im)
            out += [n_re, n_im]
        return tuple(out)

    def bwd_body(i, carry):
        k = n_tiles - 1 - i
        out = []
        for gi in range(gb):
            c_re, c_im = carry[2 * gi], carry[2 * gi + 1]
            rows, h1s_re, h1s_im, n_re, n_im = half_steps(gi, k, c_re, c_im)
            new_re = jnp.where(low_rows, h1s_re, c_re)
            new_im = jnp.where(low_rows, h1s_im, c_im)
            hc_scr[gi, rows, 0:128] = jnp.where(fwd_lanes, hc_scr[gi, rows, 0:128], new_re)
            hc_scr[gi, rows, 128:256] = jnp.where(fwd_lanes, hc_scr[gi, rows, 128:256], new_im)
            out += [n_re, n_im]
        return tuple(out)

    zeros = tuple(jnp.zeros((8, 128), F32) for _ in range(2 * gb))
    lax.fori_loop(0, n_tiles, fwd_body, zeros)
    lax.fori_loop(0, n_tiles, bwd_body, zeros)

    for gi in range(gb):
        u = u_ref[gi]
        y = (jnp.dot(u.astype(BF16), t_ref[gi], preferred_element_type=F32)
             + jnp.dot(hc_scr[gi].astype(BF16), c_ref[gi], preferred_element_type=F32)
             + d_ref[gi] * u)
        o_ref[gi] = jax.nn.gelu(y).astype(o_ref.dtype)


def _s5_scan(u_rows, ops):
    t_op, b_op, c_op, a_op, d_op = ops
    g, m, w = u_rows.shape
    gb = S5_GROUPS_PER_STEP
    spec3 = lambda shape: pl.BlockSpec((gb,) + shape, lambda i: (i,) + (0,) * len(shape))
    return pl.pallas_call(
        functools.partial(_s5_kernel, n_tiles=m // 8),
        grid=(g // gb,),
        in_specs=[spec3((m, w)), spec3((w, w)), spec3((w, w)), spec3((w, w)),
                  spec3((2, 8, 128)), spec3((1, w))],
        out_specs=spec3((m, w)),
        out_shape=jax.ShapeDtypeStruct((g, m, w), F32),
        scratch_shapes=[pltpu.VMEM((gb, m, w), F32), pltpu.VMEM((gb, m, w), F32)],
        compiler_params=_cparams("parallel"),
        name="s5_scan",
    )(u_rows, t_op, b_op, c_op, a_op, d_op)


def _proj_ln_kernel(h_ref, x_ref, w_ref, lg_ref, lb_ref, o_ref):
    h = jnp.dot(h_ref[...], w_ref[...], preferred_element_type=F32)
    o_ref[...] = _layer_norm(DN_ALPHA * x_ref[...] + h, lg_ref[...], lb_ref[...])


def _residual_ln(body, h, x, w, ln_g, ln_b, name):
    n, d = x.shape
    tile = min(ROW_TILE, n)
    row = lambda width: pl.BlockSpec((tile, width), lambda i: (i, 0))
    const = lambda shape: pl.BlockSpec(shape, lambda i: (0, 0))
    return pl.pallas_call(
        body,
        grid=(n // tile,),
        in_specs=[row(h.shape[1]), row(d), const(w.shape), const((1, d)), const((1, d))],
        out_specs=row(d),
        out_shape=jax.ShapeDtypeStruct((n, d), F32),
        compiler_params=_cparams("parallel"),
        name=name,
    )(h, x, w, ln_g.reshape(1, d), ln_b.reshape(1, d))


def _block_transpose(x):
    rows, lanes = x.ndim - 2, x.ndim - 1
    row = lax.broadcasted_iota(I32, x.shape, rows)
    blk = lax.broadcasted_iota(I32, x.shape, lanes) // S5_GROUP
    for d in (1, 2, 4):
        up = pltpu.roll(pltpu.roll(x, 8 - d, rows), S5_GROUP * d, lanes)
        down = pltpu.roll(pltpu.roll(x, d, rows), 128 - S5_GROUP * d, lanes)
        col_bit = (blk & d) != 0
        x = jnp.where((row & d) == (blk & d), x, jnp.where(col_bit, up, down))
    return x


def _s5_in_kernel(x_ref, w_ref, o_ref):
    bsz, steps, d = x_ref.shape
    n_chunk = steps // S5_CHUNK
    u = jnp.dot(x_ref[...].reshape(bsz * steps, d).astype(BF16), w_ref[...], preferred_element_type=F32)
    u = u.reshape(bsz, n_chunk, S5_CHUNK, d)
    for s in range(S5_CHUNK // 8):
        for c in range(d // 128):
            w = _block_transpose(u[:, :, s * 8:(s + 1) * 8, c * 128:(c + 1) * 128])
            for jj in range(n_chunk):
                for b in range(bsz):
                    o_ref[c * 8:(c + 1) * 8, jj * bsz + b, s * 128:(s + 1) * 128] = w[b, jj]


def _s5_glu_ln_kernel(g_ref, x_ref, w_ref, lg_ref, lb_ref, o_ref, tok_scr):
    bsz, steps, d = x_ref.shape
    n_chunk = steps // S5_CHUNK
    for s in range(S5_CHUNK // 8):
        for c in range(d // 128):
            w = jnp.stack([jnp.stack([g_ref[c * 8:(c + 1) * 8, jj * bsz + b, s * 128:(s + 1) * 128]
                                      for jj in range(n_chunk)]) for b in range(bsz)])
            tok_scr[:, :, s * 8:(s + 1) * 8, c * 128:(c + 1) * 128] = _block_transpose(w)
    g = tok_scr[...].reshape(bsz * steps, d).astype(BF16)
    z = jnp.dot(g, w_ref[...], preferred_element_type=F32)
    h = z[:, :d] * jax.nn.sigmoid(z[:, d:])
    x = x_ref[...].reshape(bsz * steps, d)
    o_ref[...] = _layer_norm(DN_ALPHA * x + h, lg_ref[...], lb_ref[...]).reshape(bsz, steps, d)


def _s5_layer(x, w_in, ops, w_glu, ln_g, ln_b):
    bsz, seq, d = x.shape
    j = seq // S5_CHUNK
    steps = ROW_TILE // bsz
    n_chunk = steps // S5_CHUNK
    width = S5_CHUNK * S5_GROUP
    tok_spec = pl.BlockSpec((bsz, steps, d), lambda i: (0, i, 0))
    row_spec = pl.BlockSpec((S5_GROUPS, n_chunk * bsz, width), lambda i: (0, i, 0))
    const = lambda shape: pl.BlockSpec(shape, lambda i: (0, 0))
    u_rows = pl.pallas_call(
        _s5_in_kernel,
        grid=(seq // steps,),
        in_specs=[tok_spec, const((d, d))],
        out_specs=row_spec,
        out_shape=jax.ShapeDtypeStruct((S5_GROUPS, j * bsz, width), F32),
        compiler_params=_cparams("parallel"),
        name="s5_in_proj",
    )(x, w_in)
    g_rows = _s5_scan(u_rows, ops)
    return pl.pallas_call(
        _s5_glu_ln_kernel,
        grid=(seq // steps,),
        in_specs=[row_spec, tok_spec, const((d, 2 * d)), const((1, d)), const((1, d))],
        out_specs=tok_spec,
        out_shape=jax.ShapeDtypeStruct((bsz, seq, d), F32),
        scratch_shapes=[pltpu.VMEM((bsz, n_chunk, S5_CHUNK, d), F32)],
        compiler_params=_cparams("parallel"),
        name="s5_glu_ln",
    )(g_rows, x, w_glu, ln_g.reshape(1, d), ln_b.reshape(1, d))


def _na_key_col_starts():
    n_cb = GRID_W // NA_QCOLS
    return [min(max(n * NA_QCOLS - NA_WIN_COLS // 2, 0), GRID_W - NA_KCOLS) for n in range(n_cb)]


def _na_bias_table(rpb):
    exact = lax.Precision.HIGHEST
    n_cb = GRID_W // NA_QCOLS
    starts = jnp.asarray(_na_key_col_starts(), I32)[:, None, None]
    half = NA_WIN_ROWS // 2
    rl = jnp.arange(NA_ROW_BLOCK)[:, None]
    kl = jnp.arange(NA_KEY_ROWS)[None, :]
    rs_rel = jnp.stack([jnp.maximum(rl - half, 0) + 0 * kl,
                        rl - half + 0 * kl,
                        jnp.minimum(rl - half, 0) + 0 * kl])
    kr_rel = kl - half
    row_ok = (kr_rel >= rs_rel) & (kr_rel < rs_rel + NA_WIN_ROWS)
    dr = jnp.clip(kl - rl + (NA_WIN_ROWS - 1 - half), 0, 2 * NA_WIN_ROWS - 2)
    qc = jnp.arange(n_cb)[:, None, None] * NA_QCOLS + jnp.arange(NA_QCOLS)[None, :, None]
    kc = starts + jnp.arange(NA_KCOLS)[None, None, :]
    ws = jnp.clip(qc - NA_WIN_COLS // 2, 0, GRID_W - NA_WIN_COLS)
    col_ok = (kc >= ws) & (kc < ws + NA_WIN_COLS)
    dc = jnp.clip(kc - qc + NA_WIN_COLS - 1, 0, 2 * NA_WIN_COLS - 2)
    pick_dc = (dc[..., None] == jnp.arange(2 * NA_WIN_COLS - 1)).astype(F32)
    pick_dr = (dr[..., None] == jnp.arange(2 * NA_WIN_ROWS - 1)).astype(F32)
    by_col = jnp.einsum('hrc,nqkc->hrnqk', rpb.astype(F32), pick_dc, precision=exact)
    bias = jnp.einsum('hrnqk,alr->nhaqlk', by_col, pick_dr, precision=exact)
    ok = row_ok[None, :, None, :, None, :, None] & col_ok[:, None, None, None, :, None, :]
    t = jnp.where(ok, bias[:, None], NA_MASKED)
    nq = NA_ROW_BLOCK * NA_QCOLS
    nk = NA_KEY_ROWS * NA_KCOLS
    return t.reshape(n_cb, 3, NA_HEADS // 2, 2 * nq, nk)


def _qkv_kernel(x_ref, w_ref, q_ref, k_ref, v_ref):
    qkv = jnp.dot(x_ref[...].astype(BF16), w_ref[...], preferred_element_type=F32)
    q_ref[...] = (qkv[:, :D_MODEL] * (NA_HEAD_DIM ** -0.5)).astype(BF16)
    k = qkv[:, D_MODEL:2 * D_MODEL]
    v = qkv[:, 2 * D_MODEL:]
    for r in range(NA_ROW_BLOCK):
        for n, start in enumerate(_na_key_col_starts()):
            lo = r * GRID_W + start
            k_ref[r, n] = k[lo:lo + NA_KCOLS].astype(BF16)
            v_ref[r, n] = v[lo:lo + NA_KCOLS].astype(BF16)


def _na_kernel(q_ref, kp_ref, kc_ref, kn_ref, vp_ref, vc_ref, vn_ref, bias_ref, o_ref, k_scr, v_scr):
    nq = NA_ROW_BLOCK * NA_QCOLS
    quarter = 4 * NA_KCOLS
    k_scr[0:quarter] = kp_ref[...].reshape(quarter, D_MODEL)
    k_scr[quarter:3 * quarter] = kc_ref[...].reshape(2 * quarter, D_MODEL)
    k_scr[3 * quarter:] = kn_ref[...].reshape(quarter, D_MODEL)
    v_scr[0:quarter] = vp_ref[...].reshape(quarter, D_MODEL)
    v_scr[quarter:3 * quarter] = vc_ref[...].reshape(2 * quarter, D_MODEL)
    v_scr[3 * quarter:] = vn_ref[...].reshape(quarter, D_MODEL)
    q = q_ref[...].reshape(nq, D_MODEL)
    first_head = lax.broadcasted_iota(I32, (nq, 128), 1) < NA_HEAD_DIM
    zero = jnp.zeros((nq, 128), BF16)
    for hp in range(NA_HEADS // 2):
        lanes = slice(hp * 128, (hp + 1) * 128)
        q2 = q[:, lanes]
        qs = jnp.concatenate([jnp.where(first_head, q2, zero), jnp.where(first_head, zero, q2)], axis=0)
        s = lax.dot_general(qs, k_scr[:, lanes], (((1,), (1,)), ((), ())),
                            preferred_element_type=F32) + bias_ref[hp]
        m = jnp.max(s, axis=-1, keepdims=True)
        p = jnp.exp(s - m)
        l = jnp.sum(p, axis=-1, keepdims=True)
        o = jnp.dot(p.astype(BF16), v_scr[:, lanes], preferred_element_type=F32) / l
        o_ref[:, :, lanes] = jnp.where(first_head, o[:nq], o[nq:]).astype(BF16).reshape(
            NA_ROW_BLOCK, NA_QCOLS, 128)


def _na_layer(x, w_qkv, bias_table, w_o, ln_g, ln_b):
    bsz, seq, d = x.shape
    rows = seq // GRID_W
    n_cb = GRID_W // NA_QCOLS
    n_rb = rows // NA_ROW_BLOCK
    assert n_rb >= 2
    tile = NA_ROW_BLOCK * GRID_W
    q, kx, vx = pl.pallas_call(
        _qkv_kernel,
        grid=(bsz, n_rb),
        in_specs=[pl.BlockSpec((None, tile, d), lambda b, i: (b, i, 0)),
                  pl.BlockSpec((d, 3 * d), lambda b, i: (0, 0))],
        out_specs=[pl.BlockSpec((None, tile, d), lambda b, i: (b, i, 0)),
                   pl.BlockSpec((None, NA_ROW_BLOCK, n_cb, NA_KCOLS, d), lambda b, i: (b, i, 0, 0, 0)),
                   pl.BlockSpec((None, NA_ROW_BLOCK, n_cb, NA_KCOLS, d), lambda b, i: (b, i, 0, 0, 0))],
        out_shape=[jax.ShapeDtypeStruct((bsz, seq, d), BF16),
                   jax.ShapeDtypeStruct((bsz, rows, n_cb, NA_KCOLS, d), BF16),
                   jax.ShapeDtypeStruct((bsz, rows, n_cb, NA_KCOLS, d), BF16)],
        compiler_params=_cparams("parallel", "parallel"),
        name="na_qkv",
    )(x, w_qkv)

    q5 = q.reshape(bsz, rows, n_cb, NA_QCOLS, d)
    n_half = rows // 4
    half_view = lambda a: a.reshape(bsz, n_half, 4, n_cb, NA_KCOLS, d)
    full_view = lambda a: a.reshape(bsz, n_rb, NA_ROW_BLOCK, n_cb, NA_KCOLS, d)
    prev_spec = pl.BlockSpec((None, None, 4, None, NA_KCOLS, d),
                             lambda n, b, i: (b, jnp.maximum(2 * i - 1, 0), 0, n, 0, 0))
    cur_spec = pl.BlockSpec((None, None, NA_ROW_BLOCK, None, NA_KCOLS, d),
                            lambda n, b, i: (b, i, 0, n, 0, 0))
    next_spec = pl.BlockSpec((None, None, 4, None, NA_KCOLS, d),
                             lambda n, b, i: (b, jnp.minimum(2 * i + 2, n_half - 1), 0, n, 0, 0))
    kind = lambda i: jnp.where(i == 0, 0, jnp.where(i == n_rb - 1, 2, 1))
    nk = NA_KEY_ROWS * NA_KCOLS
    o5 = pl.pallas_call(
        _na_kernel,
        grid=(n_cb, bsz, n_rb),
        in_specs=[pl.BlockSpec((None, NA_ROW_BLOCK, None, NA_QCOLS, d), lambda n, b, i: (b, i, n, 0, 0)),
                  prev_spec, cur_spec, next_spec, prev_spec, cur_spec, next_spec,
                  pl.BlockSpec((None, None, NA_HEADS // 2, 2 * NA_ROW_BLOCK * NA_QCOLS, nk),
                               lambda n, b, i: (n, kind(i), 0, 0, 0))],
        out_specs=pl.BlockSpec((None, NA_ROW_BLOCK, None, NA_QCOLS, d), lambda n, b, i: (b, i, n, 0, 0)),
        out_shape=jax.ShapeDtypeStruct((bsz, rows, n_cb, NA_QCOLS, d), BF16),
        scratch_shapes=[pltpu.VMEM((nk, d), BF16), pltpu.VMEM((nk, d), BF16)],
        compiler_params=_cparams("parallel", "parallel", "parallel"),
        name="na_attn",
    )(q5, half_view(kx), full_view(kx), half_view(kx), half_view(vx), full_view(vx), half_view(vx),
      bias_table)
    o = o5.reshape(bsz * seq, d)
    return _residual_ln(_proj_ln_kernel, o, x.reshape(bsz * seq, d), w_o, ln_g, ln_b,
                        "na_out_ln").reshape(bsz, seq, d)


def _xattn_kernel(x_ref, k_ref, v_ref, wq_ref, wo_ref, lg_ref, lb_ref, wr_ref, o_ref, ob_ref, aff_ref):
    x = x_ref[...]
    q = jnp.dot(x.astype(BF16), wq_ref[...], preferred_element_type=F32).astype(BF16)
    heads = []
    for h in range(MEM_HEADS):
        lanes = slice(h * MEM_HEAD_DIM, (h + 1) * MEM_HEAD_DIM)
        s = lax.dot_general(q[:, lanes], k_ref[:, lanes], (((1,), (1,)), ((), ())),
                            preferred_element_type=F32) * (MEM_HEAD_DIM ** -0.5)
        m = jnp.max(s, axis=-1, keepdims=True)
        p = jnp.exp(s - m)
        p = p / jnp.sum(p, axis=-1, keepdims=True)
        heads.append(jnp.dot(p.astype(BF16), v_ref[:, lanes], preferred_element_type=F32).astype(BF16))
    o = jnp.concatenate(heads, axis=-1)
    r = jnp.dot(o, wo_ref[...], preferred_element_type=F32)
    y = _layer_norm(DN_ALPHA * x + r, lg_ref[...], lb_ref[...])
    o_ref[...] = y
    yb = y.astype(BF16)
    ob_ref[...] = yb
    logits = lax.dot_general(wr_ref[...], yb, (((1,), (1,)), ((), ())), preferred_element_type=F32)
    e = jnp.exp(logits - jnp.max(logits, axis=0, keepdims=True))
    aff_ref[...] = e / jnp.sum(e, axis=0, keepdims=True)


def _xattn_router(x, mem_k, mem_v, w_q, w_o, ln_g, ln_b, w_router_t):
    bsz, seq, d = x.shape
    m = mem_k.shape[1]
    tile = min(ROW_TILE, seq)
    n_t = seq // tile
    const = lambda shape: pl.BlockSpec(shape, lambda b, i: (0, 0))
    return pl.pallas_call(
        _xattn_kernel,
        grid=(bsz, n_t),
        in_specs=[pl.BlockSpec((None, tile, d), lambda b, i: (b, i, 0)),
                  pl.BlockSpec((None, m, d), lambda b, i: (b, 0, 0)),
                  pl.BlockSpec((None, m, d), lambda b, i: (b, 0, 0)),
                  const((d, d)), const((d, d)), const((1, d)), const((1, d)), const((N_EXPERTS, d))],
        out_specs=[pl.BlockSpec((None, tile, d), lambda b, i: (b, i, 0)),
                   pl.BlockSpec((None, tile, d), lambda b, i: (b, i, 0)),
                   pl.BlockSpec((N_EXPERTS, tile), lambda b, i: (0, b * n_t + i))],
        out_shape=[jax.ShapeDtypeStruct((bsz, seq, d), F32),
                   jax.ShapeDtypeStruct((bsz, seq, d), BF16),
                   jax.ShapeDtypeStruct((N_EXPERTS, bsz * seq), F32)],
        compiler_params=_cparams("parallel", "parallel"),
        name="xattn_router",
    )(x, mem_k, mem_v, w_q, w_o, ln_g.reshape(1, d), ln_b.reshape(1, d), w_router_t)


def _route_kernel(aff_ref, slot_ref, r0_ref, *, cap):
    n = aff_ref.shape[1]
    n_blk = n // ROUTE_TILE
    aff = aff_ref[...]

    def count(mask):
        return jnp.sum(jnp.where(mask, 1.0, 0.0), axis=1, keepdims=True)

    def search(i, bits):
        cand = bits | lax.shift_left(jnp.int32(1), 30 - i)
        ge = aff >= lax.bitcast_convert_type(cand, F32)
        return jnp.where(count(ge) >= cap, cand, bits)

    tau = lax.bitcast_convert_type(
        lax.fori_loop(0, 31, search, jnp.zeros((N_EXPERTS, 1), I32)), F32)
    need = cap - count(aff > tau)
    upper = (lax.broadcasted_iota(I32, (ROUTE_TILE, ROUTE_TILE), 0)
             < lax.broadcasted_iota(I32, (ROUTE_TILE, ROUTE_TILE), 1)).astype(BF16)
    blk_lane = lax.broadcasted_iota(I32, (N_EXPERTS, n_blk), 1)

    r0_ref[...] = jnp.zeros_like(r0_ref)

    def block(kb, carry):
        c_eq, c_sel = carry
        cols = pl.ds(pl.multiple_of(kb * ROUTE_TILE, ROUTE_TILE), ROUTE_TILE)
        b = aff_ref[:, cols]
        eq = b == tau
        eq_f = jnp.where(eq, 1.0, 0.0)
        eq_rank = c_eq + jnp.dot(eq_f.astype(BF16), upper, preferred_element_type=F32)
        sel = (b > tau) | (eq & (eq_rank < need))
        sel_f = jnp.where(sel, 1.0, 0.0)
        rank = c_sel + jnp.dot(sel_f.astype(BF16), upper, preferred_element_type=F32)
        slot_ref[:, cols] = jnp.where(sel, rank.astype(I32), -1)
        r0_ref[...] = jnp.where(blk_lane == kb, c_sel.astype(I32), r0_ref[...])
        return (c_eq + jnp.sum(eq_f, axis=1, keepdims=True),
                c_sel + jnp.sum(sel_f, axis=1, keepdims=True))

    zero = jnp.zeros((N_EXPERTS, 1), F32)
    lax.fori_loop(0, n_blk, block, (zero, zero))


def _route(aff_t, cap):
    e, n = aff_t.shape
    n_blk = n // ROUTE_TILE
    return pl.pallas_call(
        functools.partial(_route_kernel, cap=cap),
        out_shape=[jax.ShapeDtypeStruct((e, n), I32), jax.ShapeDtypeStruct((e, n_blk), I32)],
        compiler_params=pltpu.CompilerParams(vmem_limit_bytes=VMEM_LIMIT),
        name="route",
    )(aff_t)


def _window_start(r0, cap, width):
    start = jnp.minimum((r0 // ROUTE_ALIGN) * ROUTE_ALIGN, cap - width)
    return pl.multiple_of(start, ROUTE_ALIGN)


def _tile_slots(r0_ref, e, blk, n_blk, cap):
    r0 = r0_ref[e, blk]
    nxt = r0_ref[e, jnp.minimum(blk + 1, n_blk - 1)]
    return r0, jnp.where(blk + 1 < n_blk, nxt, cap)


def _fits_small(r0_ref, experts, blk, n_blk, cap):
    starts, fits = [], None
    for e in experts:
        r0, r_end = _tile_slots(r0_ref, e, blk, n_blk, cap)
        start = _window_start(r0, cap, ROUTE_SMALL_WIN)
        ok = r_end - start <= ROUTE_SMALL_WIN
        starts.append(start)
        fits = ok if fits is None else fits & ok
    return starts, fits


def _one_hot(pick):
    return jnp.where(pick, 1.0, 0.0).astype(BF16)


def _dispatch_kernel(r0_ref, x_ref, slot_ref, xs_ref, *, cap, n_blk):
    ep = pl.program_id(0)
    j = pl.program_id(1)
    n_sub = x_ref.shape[0] // ROUTE_TILE
    experts = [ep * DISPATCH_EXPERTS + ee for ee in range(DISPATCH_EXPERTS)]

    @pl.when(j == 0)
    def _():
        xs_ref[...] = jnp.zeros_like(xs_ref)

    def add_rows(ee, start, width, rows):
        win = pl.ds(start, width)
        xs_ref[ee, win, :] = xs_ref[ee, win, :] + rows.astype(BF16)

    for sb in range(n_sub):
        blk = j * n_sub + sb
        tok = slice(sb * ROUTE_TILE, (sb + 1) * ROUTE_TILE)
        starts, fits = _fits_small(r0_ref, experts, blk, n_blk, cap)

        @pl.when(fits)
        def _():
            row = lax.broadcasted_iota(I32, (ROUTE_SMALL_WIN, ROUTE_TILE), 0)
            pick = jnp.concatenate(
                [_one_hot(row == slot_ref[ee:ee + 1, tok] - starts[ee]) for ee in range(DISPATCH_EXPERTS)],
                axis=0)
            rows = jnp.dot(pick, x_ref[tok, :], preferred_element_type=F32)
            for ee in range(DISPATCH_EXPERTS):
                add_rows(ee, starts[ee], ROUTE_SMALL_WIN,
                         rows[ee * ROUTE_SMALL_WIN:(ee + 1) * ROUTE_SMALL_WIN])

        @pl.when(jnp.logical_not(fits))
        def _():
            row = lax.broadcasted_iota(I32, (ROUTE_WIN, ROUTE_TILE), 0)
            for ee in range(DISPATCH_EXPERTS):
                start = _window_start(r0_ref[experts[ee], blk], cap, ROUTE_WIN)
                pick = _one_hot(row == slot_ref[ee:ee + 1, tok] - start)
                add_rows(ee, start, ROUTE_WIN, jnp.dot(pick, x_ref[tok, :], preferred_element_type=F32))


def _dispatch(xb, slot_t, r0, cap):
    n, d = xb.shape
    n_blk = n // ROUTE_TILE
    tile = min(DISPATCH_TOKENS, n)
    grid_spec = pltpu.PrefetchScalarGridSpec(
        num_scalar_prefetch=1,
        grid=(N_EXPERTS // DISPATCH_EXPERTS, n // tile),
        in_specs=[pl.BlockSpec((tile, d), lambda e, j, r0: (j, 0)),
                  pl.BlockSpec((None, DISPATCH_EXPERTS, tile), lambda e, j, r0: (e, 0, j))],
        out_specs=pl.BlockSpec((DISPATCH_EXPERTS, cap, d), lambda e, j, r0: (e, 0, 0)),
    )
    return pl.pallas_call(
        functools.partial(_dispatch_kernel, cap=cap, n_blk=n_blk),
        grid_spec=grid_spec,
        out_shape=jax.ShapeDtypeStruct((N_EXPERTS, cap, d), BF16),
        compiler_params=_cparams("parallel", "arbitrary"),
        name="moe_dispatch",
    )(r0, xb, slot_t.reshape(N_EXPERTS // DISPATCH_EXPERTS, DISPATCH_EXPERTS, n))


def _ffn_kernel(xs_ref, w1_ref, w3_ref, w2_ref, o_ref):
    xs = xs_ref[...]
    acc = jnp.zeros((xs.shape[0], D_MODEL), F32)
    for c in range(EXPERT_FF // FF_CHUNK):
        cols = slice(c * FF_CHUNK, (c + 1) * FF_CHUNK)
        h1 = jnp.dot(xs, w1_ref[:, cols], preferred_element_type=F32)
        h3 = jnp.dot(xs, w3_ref[:, cols], preferred_element_type=F32)
        h = (jax.nn.silu(h1) * h3).astype(BF16)
        acc = acc + jnp.dot(h, w2_ref[cols, :], preferred_element_type=F32)
    o_ref[...] = acc.astype(o_ref.dtype)


def _expert_ffn(xs, w1, w3, w2):
    e, cap, d = xs.shape
    tile = min(FFN_TILE, cap)
    return pl.pallas_call(
        _ffn_kernel,
        grid=(e, cap // tile),
        in_specs=[pl.BlockSpec((None, tile, d), lambda i, c: (i, c, 0)),
                  pl.BlockSpec((None, d, EXPERT_FF), lambda i, c: (i, 0, 0)),
                  pl.BlockSpec((None, d, EXPERT_FF), lambda i, c: (i, 0, 0)),
                  pl.BlockSpec((None, EXPERT_FF, d), lambda i, c: (i, 0, 0))],
        out_specs=pl.BlockSpec((None, tile, d), lambda i, c: (i, c, 0)),
        out_shape=jax.ShapeDtypeStruct((e, cap, d), BF16),
        compiler_params=_cparams("parallel", "parallel"),
        name="moe_ffn",
    )(xs, w1, w3, w2)


def _combine_kernel(r0_ref, x_ref, slot_ref, gate_ref, lg_ref, lb_ref, ye_ref, o_ref,
                    win_ref, big_ref, acc_ref, sem, big_sem, *, cap):
    j = pl.program_id(0)
    n_blk = pl.num_programs(0)
    buf = j % 2
    experts = range(N_EXPERTS)

    def small_copy(start, b, e):
        return pltpu.make_async_copy(ye_ref.at[e, pl.ds(start, ROUTE_SMALL_WIN)], win_ref.at[b, e],
                                     sem.at[b, e])

    starts, fits = _fits_small(r0_ref, experts, j, n_blk, cap)
    nxt = jnp.minimum(j + 1, n_blk - 1)
    nxt_starts, nxt_fits = _fits_small(r0_ref, experts, nxt, n_blk, cap)

    @pl.when((j == 0) & fits)
    def _():
        for e in experts:
            small_copy(starts[e], 0, e).start()

    @pl.when((j + 1 < n_blk) & nxt_fits)
    def _():
        for e in experts:
            small_copy(nxt_starts[e], 1 - buf, e).start()

    slot = slot_ref[...]
    gate = gate_ref[...]

    def gathered(e, start, width, window):
        lane = lax.broadcasted_iota(I32, (ROUTE_TILE, width), 1)
        pick = _one_hot(lane == slot[:, e:e + 1] - start)
        return jnp.dot(pick, window, preferred_element_type=F32) * gate[:, e:e + 1]

    @pl.when(fits)
    def _():
        acc = jnp.zeros((ROUTE_TILE, D_MODEL), F32)
        for e in experts:
            small_copy(starts[e], buf, e).wait()
            acc = acc + gathered(e, starts[e], ROUTE_SMALL_WIN, win_ref[buf, e])
        acc_ref[...] = acc

    @pl.when(jnp.logical_not(fits))
    def _():
        acc = jnp.zeros((ROUTE_TILE, D_MODEL), F32)
        for e in experts:
            start = _window_start(r0_ref[e, j], cap, ROUTE_WIN)
            copy = pltpu.make_async_copy(ye_ref.at[e, pl.ds(start, ROUTE_WIN)], big_ref, big_sem)
            copy.start()
            copy.wait()
            acc = acc + gathered(e, start, ROUTE_WIN, big_ref[...])
        acc_ref[...] = acc

    o_ref[...] = _layer_norm(DN_ALPHA * x_ref[...] + acc_ref[...], lg_ref[...], lb_ref[...])


def _combine(x, slot, gate, ye, r0, ln_g, ln_b, cap):
    n, d = x.shape
    n_blk = n // ROUTE_TILE
    grid_spec = pltpu.PrefetchScalarGridSpec(
        num_scalar_prefetch=1,
        grid=(n_blk,),
        in_specs=[pl.BlockSpec((ROUTE_TILE, d), lambda j, r0: (j, 0)),
                  pl.BlockSpec((ROUTE_TILE, N_EXPERTS), lambda j, r0: (j, 0)),
                  pl.BlockSpec((ROUTE_TILE, N_EXPERTS), lambda j, r0: (j, 0)),
                  pl.BlockSpec((1, d), lambda j, r0: (0, 0)),
                  pl.BlockSpec((1, d), lambda j, r0: (0, 0)),
                  pl.BlockSpec(memory_space=pl.ANY)],
        out_specs=pl.BlockSpec((ROUTE_TILE, d), lambda j, r0: (j, 0)),
        scratch_shapes=[pltpu.VMEM((2, N_EXPERTS, ROUTE_SMALL_WIN, d), BF16),
                        pltpu.VMEM((ROUTE_WIN, d), BF16),
                        pltpu.VMEM((ROUTE_TILE, d), F32),
                        pltpu.SemaphoreType.DMA((2, N_EXPERTS)),
                        pltpu.SemaphoreType.DMA(())],
    )
    return pl.pallas_call(
        functools.partial(_combine_kernel, cap=cap),
        grid_spec=grid_spec,
        out_shape=jax.ShapeDtypeStruct((n, d), F32),
        compiler_params=_cparams("arbitrary"),
        name="moe_combine",
    )(r0, x, slot, gate, ln_g.reshape(1, d), ln_b.reshape(1, d), ye)


def _moe_layer(x, xb, aff_t, w1, w3, w2, ln_g, ln_b):
    bsz, seq, d = x.shape
    n = bsz * seq
    cap = EC_CAPACITY_FACTOR * n // N_EXPERTS
    slot_t, r0 = _route(aff_t, cap)
    xs = _dispatch(xb.reshape(n, d), slot_t, r0, cap)
    ye = _expert_ffn(xs, w1, w3, w2)
    out = _combine(x.reshape(n, d), slot_t.T, aff_t.T, ye, r0, ln_g, ln_b, cap)
    return out.reshape(bsz, seq, d)


def _trunk(x, mem, p):
    bsz = x.shape[0]
    m = mem.shape[1]
    for i in range(DEPTH):
        if i % 2 == 0:
            x = _s5_layer(x, p["a_w_in"][i // 2], p["s5_ops"][i // 2], p["a_w_glu"][i // 2],
                          p["ln_g"][i, 0], p["ln_b"][i, 0])
        else:
            x = _na_layer(x, p["b_w_qkv"][i // 2], p["na_bias"][i // 2], p["b_w_o"][i // 2],
                          p["ln_g"][i, 0], p["ln_b"][i, 0])
        kv = _matmul(mem.reshape(bsz * m, D_MODEL), p["m_w_kv"][i], BF16).reshape(bsz, m, 2 * D_MODEL)
        x, xb, aff_t = _xattn_router(x, kv[:, :, :D_MODEL], kv[:, :, D_MODEL:], p["m_w_q"][i], p["m_w_o"][i],
                                     p["ln_g"][i, 1], p["ln_b"][i, 1], p["e_w_router_t"][i])
        x = _moe_layer(x, xb, aff_t, p["e_w1"][i], p["e_w3"][i], p["e_w2"][i],
                       p["ln_g"][i, 2], p["ln_b"][i, 2])
    return x


def kernel(x_prompt, x_sample, mem_prompt, mem_sample, a_w_in, a_lam_re, a_lam_im, a_log_dt, a_b_re, a_b_im, a_c_re, a_c_im, a_d, a_w_glu, b_w_qkv, b_rpb, b_w_o, m_w_q, m_w_kv, m_w_o, e_w_router, e_w1, e_w3, e_w2, ln_g, ln_b):
    bf = lambda w: w.astype(BF16)
    p = {
        "a_w_in": bf(a_w_in), "a_w_glu": bf(a_w_glu),
        "s5_ops": [_s5_operators(a_lam_re[j], a_lam_im[j], a_log_dt[j], a_b_re[j], a_b_im[j],
                                 a_c_re[j], a_c_im[j], a_d[j]) for j in range(a_w_in.shape[0])],
        "b_w_qkv": bf(b_w_qkv), "b_w_o": bf(b_w_o),
        "na_bias": [_na_bias_table(b_rpb[j]) for j in range(b_rpb.shape[0])],
        "m_w_q": bf(m_w_q), "m_w_kv": bf(m_w_kv), "m_w_o": bf(m_w_o),
        "e_w_router_t": bf(jnp.swapaxes(e_w_router, 1, 2)),
        "e_w1": bf(e_w1), "e_w3": bf(e_w3), "e_w2": bf(e_w2),
        "ln_g": ln_g.astype(F32), "ln_b": ln_b.astype(F32),
    }
    return (_trunk(x_prompt, mem_prompt, p), _trunk(x_sample, mem_sample, p))
```

```python
import functools
import math

import jax
import jax.numpy as jnp
from jax import lax
from jax.experimental import pallas as pl
from jax.experimental.pallas import tpu as pltpu

F32 = jnp.float32
BF16 = jnp.bfloat16
I32 = jnp.int32

D_MODEL = 1024
DEPTH = 2
GRID_W = 64
S5_GROUP = 16
S5_GROUPS = D_MODEL // S5_GROUP
S5_STATE = 64
S5_CHUNK = 16
S5_GROUPS_PER_STEP = 4
NA_HEADS = 16
NA_HEAD_DIM = D_MODEL // NA_HEADS
NA_WIN_ROWS = 8
NA_WIN_COLS = 16
NA_QCOLS = 16
NA_KCOLS = 32
NA_ROW_BLOCK = 8
NA_KEY_ROWS = 16
NA_MASKED = -1e30
MEM_HEADS = 4
MEM_HEAD_DIM = D_MODEL // MEM_HEADS
N_EXPERTS = 16
EXPERT_FF = 2048
EC_CAPACITY_FACTOR = 2
ROUTE_TILE = 256
ROUTE_ALIGN = 16
ROUTE_WIN = ROUTE_TILE + ROUTE_ALIGN
ROUTE_SMALL_WIN = 128
DISPATCH_SMALL_WIN = 64
DISPATCH_EXPERTS = 2
DISPATCH_TOKENS = 2048
GATE_LANES = 128
FFN_TILE = 512
FF_CHUNK = 512
DN_ALPHA = (2.0 * DEPTH) ** 0.25
LN_EPS = 1e-5

ROW_TILE = 512
VMEM_LIMIT = 56 * 1024 * 1024


def _cparams(*sem):
    return pltpu.CompilerParams(dimension_semantics=sem, vmem_limit_bytes=VMEM_LIMIT)


def _layer_norm(v, g, b):
    mu = jnp.mean(v, axis=-1, keepdims=True)
    c = v - mu
    var = jnp.mean(c * c, axis=-1, keepdims=True)
    return c * lax.rsqrt(var + LN_EPS) * g + b


def _matmul_kernel(x_ref, w_ref, o_ref):
    o_ref[...] = jnp.dot(x_ref[...].astype(BF16), w_ref[...],
                         preferred_element_type=F32).astype(o_ref.dtype)


def _matmul(x, w, out_dtype, tile=ROW_TILE):
    n, k = x.shape
    m = w.shape[1]
    tile = min(tile, n)
    return pl.pallas_call(
        _matmul_kernel,
        grid=(n // tile,),
        in_specs=[pl.BlockSpec((tile, k), lambda i: (i, 0)),
                  pl.BlockSpec((k, m), lambda i: (0, 0))],
        out_specs=pl.BlockSpec((tile, m), lambda i: (i, 0)),
        out_shape=jax.ShapeDtypeStruct((n, m), out_dtype),
        compiler_params=_cparams("parallel"),
        name="matmul",
    )(x, w)


def _s5_operators(lam_re, lam_im, log_dt, b_re, b_im, c_re, c_im, d_skip):
    hp = lax.Precision.HIGHEST
    n = S5_CHUNK
    dt = jnp.exp(log_dt.astype(F32))[..., None]
    lr = lam_re.astype(F32) * dt
    li = lam_im.astype(F32) * dt
    k = jnp.arange(n + 1, dtype=F32)
    mag = jnp.exp(lr[..., None] * k)
    pw_re = mag * jnp.cos(li[..., None] * k)
    pw_im = mag * jnp.sin(li[..., None] * k)
    x = pw_re[..., 1] - 1.0
    y = pw_im[..., 1]
    a = lam_re.astype(F32)
    b = lam_im.astype(F32)
    den = a * a + b * b
    q_re = ((x * a + y * b) / den)[..., None]
    q_im = ((y * a - x * b) / den)[..., None]
    bb_re = q_re * b_re.astype(F32) - q_im * b_im.astype(F32)
    bb_im = q_re * b_im.astype(F32) + q_im * b_re.astype(F32)
    cr = c_re.astype(F32)[:, :, None]
    ci = c_im.astype(F32)[:, :, None]
    pr = jnp.moveaxis(pw_re, -1, 2)[:, :, :, None, :]
    pi = jnp.moveaxis(pw_im, -1, 2)[:, :, :, None, :]
    cp_re = cr * pr - ci * pi
    cp_im = cr * pi + ci * pr
    kern = (jnp.einsum('dgkcp,dgph->dgkch', cp_re, bb_re, precision=hp)
            - jnp.einsum('dgkcp,dgph->dgkch', cp_im, bb_im, precision=hp))
    sig = jnp.arange(n)[:, None]
    tau = jnp.arange(n)[None, :]
    lag_f = tau - sig
    lag_b = sig - tau
    kf = jnp.where((lag_f >= 0)[None, :, :, None, None], kern[0][:, jnp.clip(lag_f, 0, n)], 0.0)
    kb = jnp.where((lag_b >= 0)[None, :, :, None, None], kern[1][:, jnp.clip(lag_b, 0, n)], 0.0)
    t_op = jnp.transpose(kf + kb, (0, 1, 4, 2, 3)).reshape(S5_GROUPS, n * 16, n * 16)
    idx_f = (n - 1) - jnp.arange(n)
    idx_b = jnp.arange(n)

    def b_cols(pw_r, pw_i, idx, d):
        pr_ = pw_r[d][:, :, idx]
        pi_ = pw_i[d][:, :, idx]
        re = pr_[..., None] * bb_re[d][:, :, None, :] - pi_[..., None] * bb_im[d][:, :, None, :]
        im = pr_[..., None] * bb_im[d][:, :, None, :] + pi_[..., None] * bb_re[d][:, :, None, :]
        return (jnp.transpose(re, (0, 2, 3, 1)).reshape(S5_GROUPS, n * 16, S5_STATE),
                jnp.transpose(im, (0, 2, 3, 1)).reshape(S5_GROUPS, n * 16, S5_STATE))

    bf_re, bf_im = b_cols(pw_re, pw_im, idx_f, 0)
    bb_re_, bb_im_ = b_cols(pw_re, pw_im, idx_b, 1)
    b_op = jnp.concatenate([bf_re, bb_re_, bf_im, bb_im_], axis=-1)
    kf_idx = jnp.arange(n) + 1
    kb_idx = n - jnp.arange(n)

    def c_rows(d, idx):
        re = cp_re[d][:, idx]
        im = cp_im[d][:, idx]
        re = jnp.transpose(re, (0, 3, 1, 2)).reshape(S5_GROUPS, S5_STATE, n * 16)
        im = jnp.transpose(im, (0, 3, 1, 2)).reshape(S5_GROUPS, S5_STATE, n * 16)
        return re, -im

    cf_re, cf_im = c_rows(0, kf_idx)
    cb_re, cb_im = c_rows(1, kb_idx)
    c_op = jnp.concatenate([cf_re, cb_re, cf_im, cb_im], axis=1)
    a_re = jnp.concatenate([pw_re[0][..., n], pw_re[1][..., n]], axis=-1)
    a_im = jnp.concatenate([pw_im[0][..., n], pw_im[1][..., n]], axis=-1)
    a_op = jnp.stack([a_re, a_im], axis=1)
    a_op = jnp.broadcast_to(a_op[:, :, None, :], (S5_GROUPS, 2, 8, 128))
    d_op = jnp.tile(d_skip.astype(F32).reshape(S5_GROUPS, 1, 16), (1, 1, n))
    return t_op.astype(BF16), b_op.astype(BF16), c_op.astype(BF16), a_op, d_op


def _s5_kernel(u_ref, t_ref, b_ref, c_ref, a_ref, d_ref, o_ref, s_scr, hc_scr, *, n_tiles):
    gb = u_ref.shape[0]
    for gi in range(gb):
        s_scr[gi] = jnp.dot(u_ref[gi].astype(BF16), b_ref[gi], preferred_element_type=F32)

    row = lax.broadcasted_iota(I32, (8, 128), 0)
    lane = lax.broadcasted_iota(I32, (8, 128), 1)
    low_rows = row < 4
    fwd_lanes = lane < S5_STATE
    a_re = [a_ref[gi, 0] for gi in range(gb)]
    a_im = [a_ref[gi, 1] for gi in range(gb)]

    def cmul_add(gi, h_re, h_im, v_re, v_im):
        return (a_re[gi] * h_re - a_im[gi] * h_im + v_re,
                a_re[gi] * h_im + a_im[gi] * h_re + v_im)

    def half_steps(gi, k, c_re, c_im):
        rows = pl.ds(pl.multiple_of(k * 8, 8), 8)
        v_re = s_scr[gi, rows, 0:128]
        v_im = s_scr[gi, rows, 128:256]
        h1_re, h1_im = cmul_add(gi, c_re, c_im, v_re, v_im)
        h1s_re = pltpu.roll(h1_re, 4, 0)
        h1s_im = pltpu.roll(h1_im, 4, 0)
        h2_re, h2_im = cmul_add(gi, h1s_re, h1s_im, v_re, v_im)
        return rows, h1s_re, h1s_im, pltpu.roll(h2_re, 4, 0), pltpu.roll(h2_im, 4, 0)

    def fwd_body(k, carry):
        out = []
        for gi in range(gb):
            c_re, c_im = carry[2 * gi], carry[2 * gi + 1]
            rows, h1s_re, h1s_im, n_re, n_im = half_steps(gi, k, c_re, c_im)
            hc_scr[gi, rows, 0:128] = jnp.where(low_rows, c_re, h1s_re)
            hc_scr[gi, rows, 128:256] = jnp.where(low_rows, c_im, h1s_im)
            out += [n_re, n_im]
        return tuple(out)

    def bwd_body(i, carry):
        k = n_tiles - 1 - i
        out = []
        for gi in range(gb):
            c_re, c_im = carry[2 * gi], carry[2 * gi + 1]
            rows, h1s_re, h1s_im, n_re, n_im = half_steps(gi, k, c_re, c_im)
            new_re = jnp.where(low_rows, h1s_re, c_re)
            new_im = jnp.where(low_rows, h1s_im, c_im)
            hc_scr[gi, rows, 0:128] = jnp.where(fwd_lanes, hc_scr[gi, rows, 0:128], new_re)
            hc_scr[gi, rows, 128:256] = jnp.where(fwd_lanes, hc_scr[gi, rows, 128:256], new_im)
            out += [n_re, n_im]
        return tuple(out)

    zeros = tuple(jnp.zeros((8, 128), F32) for _ in range(2 * gb))
    lax.fori_loop(0, n_tiles, fwd_body, zeros)
    lax.fori_loop(0, n_tiles, bwd_body, zeros)

    for gi in range(gb):
        u = u_ref[gi]
        y = (jnp.dot(u.astype(BF16), t_ref[gi], preferred_element_type=F32)
             + jnp.dot(hc_scr[gi].astype(BF16), c_ref[gi], preferred_element_type=F32)
             + d_ref[gi] * u)
        o_ref[gi] = jax.nn.gelu(y).astype(o_ref.dtype)


def _s5_scan(u_rows, ops):
    t_op, b_op, c_op, a_op, d_op = ops
    g, m, w = u_rows.shape
    gb = S5_GROUPS_PER_STEP
    spec3 = lambda shape: pl.BlockSpec((gb,) + shape, lambda i: (i,) + (0,) * len(shape))
    return pl.pallas_call(
        functools.partial(_s5_kernel, n_tiles=m // 8),
        grid=(g // gb,),
        in_specs=[spec3((m, w)), spec3((w, w)), spec3((w, w)), spec3((w, w)),
                  spec3((2, 8, 128)), spec3((1, w))],
        out_specs=spec3((m, w)),
        out_shape=jax.ShapeDtypeStruct((g, m, w), F32),
        scratch_shapes=[pltpu.VMEM((gb, m, w), F32), pltpu.VMEM((gb, m, w), F32)],
        compiler_params=_cparams("parallel"),
        name="s5_scan",
    )(u_rows, t_op, b_op, c_op, a_op, d_op)


def _proj_ln_kernel(h_ref, x_ref, w_ref, lg_ref, lb_ref, o_ref):
    h = jnp.dot(h_ref[...], w_ref[...], preferred_element_type=F32)
    o_ref[...] = _layer_norm(DN_ALPHA * x_ref[...] + h, lg_ref[...], lb_ref[...])


def _residual_ln(body, h, x, w, ln_g, ln_b, name):
    n, d = x.shape
    tile = min(ROW_TILE, n)
    row = lambda width: pl.BlockSpec((tile, width), lambda i: (i, 0))
    const = lambda shape: pl.BlockSpec(shape, lambda i: (0, 0))
    return pl.pallas_call(
        body,
        grid=(n // tile,),
        in_specs=[row(h.shape[1]), row(d), const(w.shape), const((1, d)), const((1, d))],
        out_specs=row(d),
        out_shape=jax.ShapeDtypeStruct((n, d), F32),
        compiler_params=_cparams("parallel"),
        name=name,
    )(h, x, w, ln_g.reshape(1, d), ln_b.reshape(1, d))


def _block_transpose(x):
    rows, lanes = x.ndim - 2, x.ndim - 1
    row = lax.broadcasted_iota(I32, x.shape, rows)
    blk = lax.broadcasted_iota(I32, x.shape, lanes) // S5_GROUP
    for d in (1, 2, 4):
        up = pltpu.roll(pltpu.roll(x, 8 - d, rows), S5_GROUP * d, lanes)
        down = pltpu.roll(pltpu.roll(x, d, rows), 128 - S5_GROUP * d, lanes)
        col_bit = (blk & d) != 0
        x = jnp.where((row & d) == (blk & d), x, jnp.where(col_bit, up, down))
    return x


def _s5_in_kernel(x_ref, w_ref, o_ref):
    bsz, steps, d = x_ref.shape
    n_chunk = steps // S5_CHUNK
    u = jnp.dot(x_ref[...].reshape(bsz * steps, d).astype(BF16), w_ref[...], preferred_element_type=F32)
    u = u.reshape(bsz, n_chunk, S5_CHUNK, d)
    for s in range(S5_CHUNK // 8):
        for c in range(d // 128):
            w = _block_transpose(u[:, :, s * 8:(s + 1) * 8, c * 128:(c + 1) * 128])
            for jj in range(n_chunk):
                for b in range(bsz):
                    o_ref[c * 8:(c + 1) * 8, jj * bsz + b, s * 128:(s + 1) * 128] = w[b, jj]


def _s5_glu_ln_kernel(g_ref, x_ref, w_ref, lg_ref, lb_ref, o_ref, tok_scr):
    bsz, steps, d = x_ref.shape
    n_chunk = steps // S5_CHUNK
    for s in range(S5_CHUNK // 8):
        for c in range(d // 128):
            w = jnp.stack([jnp.stack([g_ref[c * 8:(c + 1) * 8, jj * bsz + b, s * 128:(s + 1) * 128]
                                      for jj in range(n_chunk)]) for b in range(bsz)])
            tok_scr[:, :, s * 8:(s + 1) * 8, c * 128:(c + 1) * 128] = _block_transpose(w)
    g = tok_scr[...].reshape(bsz * steps, d).astype(BF16)
    z = jnp.dot(g, w_ref[...], preferred_element_type=F32)
    h = z[:, :d] * jax.nn.sigmoid(z[:, d:])
    x = x_ref[...].reshape(bsz * steps, d)
    o_ref[...] = _layer_norm(DN_ALPHA * x + h, lg_ref[...], lb_ref[...]).reshape(bsz, steps, d)


def _s5_layer(x, w_in, ops, w_glu, ln_g, ln_b):
    bsz, seq, d = x.shape
    j = seq // S5_CHUNK
    steps = ROW_TILE // bsz
    n_chunk = steps // S5_CHUNK
    width = S5_CHUNK * S5_GROUP
    tok_spec = pl.BlockSpec((bsz, steps, d), lambda i: (0, i, 0))
    row_spec = pl.BlockSpec((S5_GROUPS, n_chunk * bsz, width), lambda i: (0, i, 0))
    const = lambda shape: pl.BlockSpec(shape, lambda i: (0, 0))
    u_rows = pl.pallas_call(
        _s5_in_kernel,
        grid=(seq // steps,),
        in_specs=[tok_spec, const((d, d))],
        out_specs=row_spec,
        out_shape=jax.ShapeDtypeStruct((S5_GROUPS, j * bsz, width), F32),
        compiler_params=_cparams("parallel"),
        name="s5_in_proj",
    )(x, w_in)
    g_rows = _s5_scan(u_rows, ops)
    return pl.pallas_call(
        _s5_glu_ln_kernel,
        grid=(seq // steps,),
        in_specs=[row_spec, tok_spec, const((d, 2 * d)), const((1, d)), const((1, d))],
        out_specs=tok_spec,
        out_shape=jax.ShapeDtypeStruct((bsz, seq, d), F32),
        scratch_shapes=[pltpu.VMEM((bsz, n_chunk, S5_CHUNK, d), F32)],
        compiler_params=_cparams("parallel"),
        name="s5_glu_ln",
    )(g_rows, x, w_glu, ln_g.reshape(1, d), ln_b.reshape(1, d))


def _na_key_col_starts():
    n_cb = GRID_W // NA_QCOLS
    return [min(max(n * NA_QCOLS - NA_WIN_COLS // 2, 0), GRID_W - NA_KCOLS) for n in range(n_cb)]


def _na_bias_table(rpb):
    exact = lax.Precision.HIGHEST
    n_cb = GRID_W // NA_QCOLS
    starts = jnp.asarray(_na_key_col_starts(), I32)[:, None, None]
    half = NA_WIN_ROWS // 2
    rl = jnp.arange(NA_ROW_BLOCK)[:, None]
    kl = jnp.arange(NA_KEY_ROWS)[None, :]
    rs_rel = jnp.stack([jnp.maximum(rl - half, 0) + 0 * kl,
                        rl - half + 0 * kl,
                        jnp.minimum(rl - half, 0) + 0 * kl])
    kr_rel = kl - half
    row_ok = (kr_rel >= rs_rel) & (kr_rel < rs_rel + NA_WIN_ROWS)
    dr = jnp.clip(kl - rl + (NA_WIN_ROWS - 1 - half), 0, 2 * NA_WIN_ROWS - 2)
    qc = jnp.arange(n_cb)[:, None, None] * NA_QCOLS + jnp.arange(NA_QCOLS)[None, :, None]
    kc = starts + jnp.arange(NA_KCOLS)[None, None, :]
    ws = jnp.clip(qc - NA_WIN_COLS // 2, 0, GRID_W - NA_WIN_COLS)
    col_ok = (kc >= ws) & (kc < ws + NA_WIN_COLS)
    dc = jnp.clip(kc - qc + NA_WIN_COLS - 1, 0, 2 * NA_WIN_COLS - 2)
    pick_dc = (dc[..., None] == jnp.arange(2 * NA_WIN_COLS - 1)).astype(F32)
    pick_dr = (dr[..., None] == jnp.arange(2 * NA_WIN_ROWS - 1)).astype(F32)
    by_col = jnp.einsum('hrc,nqkc->hrnqk', rpb.astype(F32), pick_dc, precision=exact)
    bias = jnp.einsum('hrnqk,alr->nhaqlk', by_col, pick_dr, precision=exact)
    ok = row_ok[None, :, None, :, None, :, None] & col_ok[:, None, None, None, :, None, :]
    t = jnp.where(ok, bias[:, None], NA_MASKED)
    nq = NA_ROW_BLOCK * NA_QCOLS
    nk = NA_KEY_ROWS * NA_KCOLS
    return t.reshape(n_cb, 3, NA_HEADS // 2, 2 * nq, nk)


def _qkv_kernel(x_ref, w_ref, q_ref, k_ref, v_ref):
    qkv = jnp.dot(x_ref[...].astype(BF16), w_ref[...], preferred_element_type=F32)
    q_ref[...] = (qkv[:, :D_MODEL] * (NA_HEAD_DIM ** -0.5)).astype(BF16)
    k = qkv[:, D_MODEL:2 * D_MODEL]
    v = qkv[:, 2 * D_MODEL:]
    for r in range(NA_ROW_BLOCK):
        for n, start in enumerate(_na_key_col_starts()):
            lo = r * GRID_W + start
            k_ref[r, n] = k[lo:lo + NA_KCOLS].astype(BF16)
            v_ref[r, n] = v[lo:lo + NA_KCOLS].astype(BF16)


def _na_kernel(q_ref, kp_ref, kc_ref, kn_ref, vp_ref, vc_ref, vn_ref, bias_ref, o_ref, k_scr, v_scr):
    nq = NA_ROW_BLOCK * NA_QCOLS
    quarter = 4 * NA_KCOLS
    k_scr[0:quarter] = kp_ref[...].reshape(quarter, D_MODEL)
    k_scr[quarter:3 * quarter] = kc_ref[...].reshape(2 * quarter, D_MODEL)
    k_scr[3 * quarter:] = kn_ref[...].reshape(quarter, D_MODEL)
    v_scr[0:quarter] = vp_ref[...].reshape(quarter, D_MODEL)
    v_scr[quarter:3 * quarter] = vc_ref[...].reshape(2 * quarter, D_MODEL)
    v_scr[3 * quarter:] = vn_ref[...].reshape(quarter, D_MODEL)
    q = q_ref[...].reshape(nq, D_MODEL)
    first_head = lax.broadcasted_iota(I32, (nq, 128), 1) < NA_HEAD_DIM
    zero = jnp.zeros((nq, 128), BF16)
    for hp in range(NA_HEADS // 2):
        lanes = slice(hp * 128, (hp + 1) * 128)
        q2 = q[:, lanes]
        qs = jnp.concatenate([jnp.where(first_head, q2, zero), jnp.where(first_head, zero, q2)], axis=0)
        s = lax.dot_general(qs, k_scr[:, lanes], (((1,), (1,)), ((), ())),
                            preferred_element_type=F32) + bias_ref[hp]
        m = jnp.max(s, axis=-1, keepdims=True)
        p = jnp.exp(s - m)
        l = jnp.sum(p, axis=-1, keepdims=True)
        o = jnp.dot(p.astype(BF16), v_scr[:, lanes], preferred_element_type=F32) / l
        o_ref[:, :, lanes] = jnp.where(first_head, o[:nq], o[nq:]).astype(BF16).reshape(
            NA_ROW_BLOCK, NA_QCOLS, 128)


def _na_layer(x, w_qkv, bias_table, w_o, ln_g, ln_b):
    bsz, seq, d = x.shape
    rows = seq // GRID_W
    n_cb = GRID_W // NA_QCOLS
    n_rb = rows // NA_ROW_BLOCK
    assert n_rb >= 2
    tile = NA_ROW_BLOCK * GRID_W
    q, kx, vx = pl.pallas_call(
        _qkv_kernel,
        grid=(bsz, n_rb),
        in_specs=[pl.BlockSpec((None, tile, d), lambda b, i: (b, i, 0)),
                  pl.BlockSpec((d, 3 * d), lambda b, i: (0, 0))],
        out_specs=[pl.BlockSpec((None, tile, d), lambda b, i: (b, i, 0)),
                   pl.BlockSpec((None, NA_ROW_BLOCK, n_cb, NA_KCOLS, d), lambda b, i: (b, i, 0, 0, 0)),
                   pl.BlockSpec((None, NA_ROW_BLOCK, n_cb, NA_KCOLS, d), lambda b, i: (b, i, 0, 0, 0))],
        out_shape=[jax.ShapeDtypeStruct((bsz, seq, d), BF16),
                   jax.ShapeDtypeStruct((bsz, rows, n_cb, NA_KCOLS, d), BF16),
                   jax.ShapeDtypeStruct((bsz, rows, n_cb, NA_KCOLS, d), BF16)],
        compiler_params=_cparams("parallel", "parallel"),
        name="na_qkv",
    )(x, w_qkv)

    q5 = q.reshape(bsz, rows, n_cb, NA_QCOLS, d)
    n_half = rows // 4
    half_view = lambda a: a.reshape(bsz, n_half, 4, n_cb, NA_KCOLS, d)
    full_view = lambda a: a.reshape(bsz, n_rb, NA_ROW_BLOCK, n_cb, NA_KCOLS, d)
    prev_spec = pl.BlockSpec((None, None, 4, None, NA_KCOLS, d),
                             lambda n, b, i: (b, jnp.maximum(2 * i - 1, 0), 0, n, 0, 0))
    cur_spec = pl.BlockSpec((None, None, NA_ROW_BLOCK, None, NA_KCOLS, d),
                            lambda n, b, i: (b, i, 0, n, 0, 0))
    next_spec = pl.BlockSpec((None, None, 4, None, NA_KCOLS, d),
                             lambda n, b, i: (b, jnp.minimum(2 * i + 2, n_half - 1), 0, n, 0, 0))
    kind = lambda i: jnp.where(i == 0, 0, jnp.where(i == n_rb - 1, 2, 1))
    nk = NA_KEY_ROWS * NA_KCOLS
    o5 = pl.pallas_call(
        _na_kernel,
        grid=(n_cb, bsz, n_rb),
        in_specs=[pl.BlockSpec((None, NA_ROW_BLOCK, None, NA_QCOLS, d), lambda n, b, i: (b, i, n, 0, 0)),
                  prev_spec, cur_spec, next_spec, prev_spec, cur_spec, next_spec,
                  pl.BlockSpec((None, None, NA_HEADS // 2, 2 * NA_ROW_BLOCK * NA_QCOLS, nk),
                               lambda n, b, i: (n, kind(i), 0, 0, 0))],
        out_specs=pl.BlockSpec((None, NA_ROW_BLOCK, None, NA_QCOLS, d), lambda n, b, i: (b, i, n, 0, 0)),
        out_shape=jax.ShapeDtypeStruct((bsz, rows, n_cb, NA_QCOLS, d), BF16),
        scratch_shapes=[pltpu.VMEM((nk, d), BF16), pltpu.VMEM((nk, d), BF16)],
        compiler_params=_cparams("parallel", "parallel", "parallel"),
        name="na_attn",
    )(q5, half_view(kx), full_view(kx), half_view(kx), half_view(vx), full_view(vx), half_view(vx),
      bias_table)
    o = o5.reshape(bsz * seq, d)
    return _residual_ln(_proj_ln_kernel, o, x.reshape(bsz * seq, d), w_o, ln_g, ln_b,
                        "na_out_ln").reshape(bsz, seq, d)


def _xattn_kernel(x_ref, k_ref, v_ref, wq_ref, wo_ref, lg_ref, lb_ref, wr_ref, o_ref, ob_ref, aff_ref):
    x = x_ref[...]
    q = jnp.dot(x.astype(BF16), wq_ref[...], preferred_element_type=F32).astype(BF16)
    heads = []
    for h in range(MEM_HEADS):
        lanes = slice(h * MEM_HEAD_DIM, (h + 1) * MEM_HEAD_DIM)
        s = lax.dot_general(q[:, lanes], k_ref[:, lanes], (((1,), (1,)), ((), ())),
                            preferred_element_type=F32) * (MEM_HEAD_DIM ** -0.5)
        m = jnp.max(s, axis=-1, keepdims=True)
        p = jnp.exp(s - m)
        p = p / jnp.sum(p, axis=-1, keepdims=True)
        heads.append(jnp.dot(p.astype(BF16), v_ref[:, lanes], preferred_element_type=F32).astype(BF16))
    o = jnp.concatenate(heads, axis=-1)
    r = jnp.dot(o, wo_ref[...], preferred_element_type=F32)
    y = _layer_norm(DN_ALPHA * x + r, lg_ref[...], lb_ref[...])
    o_ref[...] = y
    yb = y.astype(BF16)
    ob_ref[...] = yb
    logits = lax.dot_general(wr_ref[...], yb, (((1,), (1,)), ((), ())), preferred_element_type=F32)
    e = jnp.exp(logits - jnp.max(logits, axis=0, keepdims=True))
    aff_ref[...] = e / jnp.sum(e, axis=0, keepdims=True)


def _xattn_router(x, mem_k, mem_v, w_q, w_o, ln_g, ln_b, w_router_t):
    bsz, seq, d = x.shape
    m = mem_k.shape[1]
    tile = min(ROW_TILE, seq)
    n_t = seq // tile
    const = lambda shape: pl.BlockSpec(shape, lambda b, i: (0, 0))
    return pl.pallas_call(
        _xattn_kernel,
        grid=(bsz, n_t),
        in_specs=[pl.BlockSpec((None, tile, d), lambda b, i: (b, i, 0)),
                  pl.BlockSpec((None, m, d), lambda b, i: (b, 0, 0)),
                  pl.BlockSpec((None, m, d), lambda b, i: (b, 0, 0)),
                  const((d, d)), const((d, d)), const((1, d)), const((1, d)), const((N_EXPERTS, d))],
        out_specs=[pl.BlockSpec((None, tile, d), lambda b, i: (b, i, 0)),
                   pl.BlockSpec((None, tile, d), lambda b, i: (b, i, 0)),
                   pl.BlockSpec((N_EXPERTS, tile), lambda b, i: (0, b * n_t + i))],
        out_shape=[jax.ShapeDtypeStruct((bsz, seq, d), F32),
                   jax.ShapeDtypeStruct((bsz, seq, d), BF16),
                   jax.ShapeDtypeStruct((N_EXPERTS, bsz * seq), F32)],
        compiler_params=_cparams("parallel", "parallel"),
        name="xattn_router",
    )(x, mem_k, mem_v, w_q, w_o, ln_g.reshape(1, d), ln_b.reshape(1, d), w_router_t)


def _route_kernel(aff_ref, slot_ref, r0_ref, *, cap):
    n = aff_ref.shape[1]
    n_blk = n // ROUTE_TILE
    aff = aff_ref[...]

    def count(mask):
        return jnp.sum(jnp.where(mask, 1.0, 0.0), axis=1, keepdims=True)

    def search(i, bits):
        cand = bits | lax.shift_left(jnp.int32(1), 30 - i)
        ge = aff >= lax.bitcast_convert_type(cand, F32)
        return jnp.where(count(ge) >= cap, cand, bits)

    tau = lax.bitcast_convert_type(
        lax.fori_loop(0, 31, search, jnp.zeros((N_EXPERTS, 1), I32)), F32)
    need = cap - count(aff > tau)
    upper = (lax.broadcasted_iota(I32, (ROUTE_TILE, ROUTE_TILE), 0)
             < lax.broadcasted_iota(I32, (ROUTE_TILE, ROUTE_TILE), 1)).astype(BF16)
    blk_lane = lax.broadcasted_iota(I32, (N_EXPERTS, n_blk), 1)

    r0_ref[...] = jnp.zeros_like(r0_ref)

    def block(kb, carry):
        c_eq, c_sel = carry
        cols = pl.ds(pl.multiple_of(kb * ROUTE_TILE, ROUTE_TILE), ROUTE_TILE)
        b = aff_ref[:, cols]
        eq = b == tau
        eq_f = jnp.where(eq, 1.0, 0.0)
        eq_rank = c_eq + jnp.dot(eq_f.astype(BF16), upper, preferred_element_type=F32)
        sel = (b > tau) | (eq & (eq_rank < need))
        sel_f = jnp.where(sel, 1.0, 0.0)
        rank = c_sel + jnp.dot(sel_f.astype(BF16), upper, preferred_element_type=F32)
        slot_ref[:, cols] = jnp.where(sel, rank.astype(I32), -1)
        r0_ref[...] = jnp.where(blk_lane == kb, c_sel.astype(I32), r0_ref[...])
        return (c_eq + jnp.sum(eq_f, axis=1, keepdims=True),
                c_sel + jnp.sum(sel_f, axis=1, keepdims=True))

    zero = jnp.zeros((N_EXPERTS, 1), F32)
    lax.fori_loop(0, n_blk, block, (zero, zero))


def _route(aff_t, cap):
    e, n = aff_t.shape
    n_blk = n // ROUTE_TILE
    return pl.pallas_call(
        functools.partial(_route_kernel, cap=cap),
        out_shape=[jax.ShapeDtypeStruct((e, n), I32), jax.ShapeDtypeStruct((e, n_blk), I32)],
        compiler_params=pltpu.CompilerParams(vmem_limit_bytes=VMEM_LIMIT),
        name="route",
    )(aff_t)


def _window_start(r0, cap, width):
    start = jnp.minimum((r0 // ROUTE_ALIGN) * ROUTE_ALIGN, cap - width)
    return pl.multiple_of(start, ROUTE_ALIGN)


def _tile_slots(r0_ref, e, blk, n_blk, cap):
    r0 = r0_ref[e, blk]
    nxt = r0_ref[e, jnp.minimum(blk + 1, n_blk - 1)]
    return r0, jnp.where(blk + 1 < n_blk, nxt, cap)


def _fits_small(r0_ref, experts, blk, n_blk, cap, width):
    starts, fits = [], None
    for e in experts:
        r0, r_end = _tile_slots(r0_ref, e, blk, n_blk, cap)
        start = _window_start(r0, cap, width)
        ok = r_end - start <= width
        starts.append(start)
        fits = ok if fits is None else fits & ok
    return starts, fits


def _one_hot(pick):
    return jnp.where(pick, 1.0, 0.0).astype(BF16)


def _split3(g):
    hi = g.astype(BF16).astype(F32)
    mid = (g - hi).astype(BF16).astype(F32)
    return hi, mid, (g - hi - mid).astype(BF16).astype(F32)


def _dispatch_kernel(r0_ref, x_ref, slot_ref, gate_ref, xs_ref, *, cap, n_blk):
    ep = pl.program_id(0)
    j = pl.program_id(1)
    n_sub = x_ref.shape[0] // ROUTE_TILE
    experts = [ep * DISPATCH_EXPERTS + ee for ee in range(DISPATCH_EXPERTS)]

    @pl.when(j == 0)
    def _():
        xs_ref[...] = jnp.zeros_like(xs_ref)

    def scatter(ee, tok, start, width, pick, rows):
        gate = jnp.sum(jnp.where(pick, gate_ref[ee:ee + 1, tok], 0.0), axis=1, keepdims=True)
        hi, mid, lo = _split3(gate)
        lane = lax.broadcasted_iota(I32, (width, GATE_LANES), 1)
        cols = jnp.where(lane == 0, hi, jnp.where(lane == 1, mid, jnp.where(lane == 2, lo, 0.0)))
        win = pl.ds(start, width)
        xs_ref[ee, win, :D_MODEL] = xs_ref[ee, win, :D_MODEL] + rows.astype(BF16)
        xs_ref[ee, win, D_MODEL:] = xs_ref[ee, win, D_MODEL:] + cols.astype(BF16)

    for sb in range(n_sub):
        blk = j * n_sub + sb
        tok = slice(sb * ROUTE_TILE, (sb + 1) * ROUTE_TILE)
        starts, fits = _fits_small(r0_ref, experts, blk, n_blk, cap, DISPATCH_SMALL_WIN)

        @pl.when(fits)
        def _():
            row = lax.broadcasted_iota(I32, (DISPATCH_SMALL_WIN, ROUTE_TILE), 0)
            picks = [row == slot_ref[ee:ee + 1, tok] - starts[ee] for ee in range(DISPATCH_EXPERTS)]
            rows = jnp.dot(jnp.concatenate([_one_hot(p) for p in picks], axis=0), x_ref[tok, :],
                           preferred_element_type=F32)
            for ee in range(DISPATCH_EXPERTS):
                scatter(ee, tok, starts[ee], DISPATCH_SMALL_WIN, picks[ee],
                        rows[ee * DISPATCH_SMALL_WIN:(ee + 1) * DISPATCH_SMALL_WIN])

        @pl.when(jnp.logical_not(fits))
        def _():
            row = lax.broadcasted_iota(I32, (ROUTE_WIN, ROUTE_TILE), 0)
            for ee in range(DISPATCH_EXPERTS):
                start = _window_start(r0_ref[experts[ee], blk], cap, ROUTE_WIN)
                pick = row == slot_ref[ee:ee + 1, tok] - start
                scatter(ee, tok, start, ROUTE_WIN, pick,
                        jnp.dot(_one_hot(pick), x_ref[tok, :], preferred_element_type=F32))


def _dispatch(xb, slot_t, aff_t, r0, cap):
    n, d = xb.shape
    n_blk = n // ROUTE_TILE
    tile = min(DISPATCH_TOKENS, n)
    pairs = lambda a: a.reshape(N_EXPERTS // DISPATCH_EXPERTS, DISPATCH_EXPERTS, n)
    pair_spec = pl.BlockSpec((None, DISPATCH_EXPERTS, tile), lambda e, j, r0: (e, 0, j))
    grid_spec = pltpu.PrefetchScalarGridSpec(
        num_scalar_prefetch=1,
        grid=(N_EXPERTS // DISPATCH_EXPERTS, n // tile),
        in_specs=[pl.BlockSpec((tile, d), lambda e, j, r0: (j, 0)), pair_spec, pair_spec],
        out_specs=pl.BlockSpec((DISPATCH_EXPERTS, cap, d + GATE_LANES), lambda e, j, r0: (e, 0, 0)),
    )
    return pl.pallas_call(
        functools.partial(_dispatch_kernel, cap=cap, n_blk=n_blk),
        grid_spec=grid_spec,
        out_shape=jax.ShapeDtypeStruct((N_EXPERTS, cap, d + GATE_LANES), BF16),
        compiler_params=_cparams("parallel", "arbitrary"),
        name="moe_dispatch",
    )(r0, xb, pairs(slot_t), pairs(aff_t))


def _ffn_kernel(xs_ref, w1_ref, w3_ref, w2_ref, o_ref):
    xs = xs_ref[:, :D_MODEL]
    gate = jnp.sum(xs_ref[:, D_MODEL:].astype(F32), axis=1, keepdims=True)
    acc = jnp.zeros((xs.shape[0], D_MODEL), F32)
    for c in range(EXPERT_FF // FF_CHUNK):
        cols = slice(c * FF_CHUNK, (c + 1) * FF_CHUNK)
        h1 = jnp.dot(xs, w1_ref[:, cols], preferred_element_type=F32)
        h3 = jnp.dot(xs, w3_ref[:, cols], preferred_element_type=F32)
        h = (jax.nn.silu(h1) * h3).astype(BF16)
        acc = acc + jnp.dot(h, w2_ref[cols, :], preferred_element_type=F32)
    o_ref[...] = (acc * gate).astype(o_ref.dtype)


def _expert_ffn(xs, w1, w3, w2, layer):
    e, cap, width = xs.shape
    d = D_MODEL
    tile = min(FFN_TILE, cap)
    return pl.pallas_call(
        _ffn_kernel,
        grid=(e, cap // tile),
        in_specs=[pl.BlockSpec((None, tile, width), lambda i, c: (i, c, 0)),
                  pl.BlockSpec((None, None, d, EXPERT_FF), lambda i, c: (layer, i, 0, 0)),
                  pl.BlockSpec((None, None, d, EXPERT_FF), lambda i, c: (layer, i, 0, 0)),
                  pl.BlockSpec((None, None, EXPERT_FF, d), lambda i, c: (layer, i, 0, 0))],
        out_specs=pl.BlockSpec((None, tile, d), lambda i, c: (i, c, 0)),
        out_shape=jax.ShapeDtypeStruct((e, cap, d), BF16),
        compiler_params=_cparams("parallel", "parallel"),
        name="moe_ffn",
    )(xs, w1, w3, w2)


def _combine_kernel(r0_ref, x_ref, slot_ref, lg_ref, lb_ref, ye_ref, o_ref,
                    win_ref, big_ref, acc_ref, sem, big_sem, *, cap):
    j = pl.program_id(0)
    n_blk = pl.num_programs(0)
    buf = j % 2
    experts = range(N_EXPERTS)

    def small_copy(start, b, e):
        return pltpu.make_async_copy(ye_ref.at[e, pl.ds(start, ROUTE_SMALL_WIN)],
                                     win_ref.at[b, pl.ds(e * ROUTE_SMALL_WIN, ROUTE_SMALL_WIN)],
                                     sem.at[b, e])

    starts, fits = _fits_small(r0_ref, experts, j, n_blk, cap, ROUTE_SMALL_WIN)
    nxt = jnp.minimum(j + 1, n_blk - 1)
    nxt_starts, nxt_fits = _fits_small(r0_ref, experts, nxt, n_blk, cap, ROUTE_SMALL_WIN)

    @pl.when((j == 0) & fits)
    def _():
        for e in experts:
            small_copy(starts[e], 0, e).start()

    @pl.when((j + 1 < n_blk) & nxt_fits)
    def _():
        for e in experts:
            small_copy(nxt_starts[e], 1 - buf, e).start()

    slot = slot_ref[...]

    def pick(e, start, width):
        lane = lax.broadcasted_iota(I32, (ROUTE_TILE, width), 1)
        return _one_hot(lane == slot[:, e:e + 1] - start)

    @pl.when(fits)
    def _():
        for e in experts:
            small_copy(starts[e], buf, e).wait()
        picks = jnp.concatenate([pick(e, starts[e], ROUTE_SMALL_WIN) for e in experts], axis=1)
        acc_ref[...] = jnp.dot(picks, win_ref[buf], preferred_element_type=F32)

    @pl.when(jnp.logical_not(fits))
    def _():
        acc = jnp.zeros((ROUTE_TILE, D_MODEL), F32)
        for e in experts:
            start = _window_start(r0_ref[e, j], cap, ROUTE_WIN)
            copy = pltpu.make_async_copy(ye_ref.at[e, pl.ds(start, ROUTE_WIN)], big_ref, big_sem)
            copy.start()
            copy.wait()
            acc = acc + jnp.dot(pick(e, start, ROUTE_WIN), big_ref[...], preferred_element_type=F32)
        acc_ref[...] = acc

    o_ref[...] = _layer_norm(DN_ALPHA * x_ref[...] + acc_ref[...], lg_ref[...], lb_ref[...])


def _combine(x, slot, ye, r0, ln_g, ln_b, cap):
    n, d = x.shape
    n_blk = n // ROUTE_TILE
    grid_spec = pltpu.PrefetchScalarGridSpec(
        num_scalar_prefetch=1,
        grid=(n_blk,),
        in_specs=[pl.BlockSpec((ROUTE_TILE, d), lambda j, r0: (j, 0)),
                  pl.BlockSpec((ROUTE_TILE, N_EXPERTS), lambda j, r0: (j, 0)),
                  pl.BlockSpec((1, d), lambda j, r0: (0, 0)),
                  pl.BlockSpec((1, d), lambda j, r0: (0, 0)),
                  pl.BlockSpec(memory_space=pl.ANY)],
        out_specs=pl.BlockSpec((ROUTE_TILE, d), lambda j, r0: (j, 0)),
        scratch_shapes=[pltpu.VMEM((2, N_EXPERTS * ROUTE_SMALL_WIN, d), BF16),
                        pltpu.VMEM((ROUTE_WIN, d), BF16),
                        pltpu.VMEM((ROUTE_TILE, d), F32),
                        pltpu.SemaphoreType.DMA((2, N_EXPERTS)),
                        pltpu.SemaphoreType.DMA(())],
    )
    return pl.pallas_call(
        functools.partial(_combine_kernel, cap=cap),
        grid_spec=grid_spec,
        out_shape=jax.ShapeDtypeStruct((n, d), F32),
        compiler_params=_cparams("arbitrary"),
        name="moe_combine",
    )(r0, x, slot, ln_g.reshape(1, d), ln_b.reshape(1, d), ye)


def _moe_layer(x, xb, aff_t, w1, w3, w2, layer, ln_g, ln_b):
    bsz, seq, d = x.shape
    n = bsz * seq
    cap = EC_CAPACITY_FACTOR * n // N_EXPERTS
    slot_t, r0 = _route(aff_t, cap)
    xs = _dispatch(xb.reshape(n, d), slot_t, aff_t, r0, cap)
    ye = _expert_ffn(xs, w1, w3, w2, layer)
    out = _combine(x.reshape(n, d), slot_t.T, ye, r0, ln_g, ln_b, cap)
    return out.reshape(bsz, seq, d)


def _trunk(x, mem, p):
    bsz = x.shape[0]
    m = mem.shape[1]
    for i in range(DEPTH):
        if i % 2 == 0:
            x = _s5_layer(x, p["a_w_in"][i // 2], p["s5_ops"][i // 2], p["a_w_glu"][i // 2],
                          p["ln_g"][i, 0], p["ln_b"][i, 0])
        else:
            x = _na_layer(x, p["b_w_qkv"][i // 2], p["na_bias"][i // 2], p["b_w_o"][i // 2],
                          p["ln_g"][i, 0], p["ln_b"][i, 0])
        kv = _matmul(mem.reshape(bsz * m, D_MODEL), p["m_w_kv"][i], BF16).reshape(bsz, m, 2 * D_MODEL)
        x, xb, aff_t = _xattn_router(x, kv[:, :, :D_MODEL], kv[:, :, D_MODEL:], p["m_w_q"][i], p["m_w_o"][i],
                                     p["ln_g"][i, 1], p["ln_b"][i, 1], p["e_w_router_t"][i])
        x = _moe_layer(x, xb, aff_t, p["e_w1"], p["e_w3"], p["e_w2"], i,
                       p["ln_g"][i, 2], p["ln_b"][i, 2])
    return x


def kernel(x_prompt, x_sample, mem_prompt, mem_sample, a_w_in, a_lam_re, a_lam_im, a_log_dt, a_b_re, a_b_im, a_c_re, a_c_im, a_d, a_w_glu, b_w_qkv, b_rpb, b_w_o, m_w_q, m_w_kv, m_w_o, e_w_router, e_w1, e_w3, e_w2, ln_g, ln_b):
    bf = lambda w: w.astype(BF16)
    p = {
        "a_w_in": bf(a_w_in), "a_w_glu": bf(a_w_glu),
        "s5_ops": [_s5_operators(a_lam_re[j], a_lam_im[j], a_log_dt[j], a_b_re[j], a_b_im[j],
                                 a_c_re[j], a_c_im[j], a_d[j]) for j in range(a_w_in.shape[0])],
        "b_w_qkv": bf(b_w_qkv), "b_w_o": bf(b_w_o),
        "na_bias": [_na_bias_table(b_rpb[j]) for j in range(b_rpb.shape[0])],
        "m_w_q": bf(m_w_q), "m_w_kv": bf(m_w_kv), "m_w_o": bf(m_w_o),
        "e_w_router_t": bf(jnp.swapaxes(e_w_router, 1, 2)),
        "e_w1": bf(e_w1), "e_w3": bf(e_w3), "e_w2": bf(e_w2),
        "ln_g": ln_g.astype(F32), "ln_b": ln_b.astype(F32),
    }
    return (_trunk(x_prompt, mem_prompt, p), _trunk(x_sample, mem_sample, p))
```

```python
import functools
import math

import jax
import jax.numpy as jnp
from jax import lax
from jax.experimental import pallas as pl
from jax.experimental.pallas import tpu as pltpu

F32 = jnp.float32
BF16 = jnp.bfloat16
I32 = jnp.int32

D_MODEL = 1024
DEPTH = 2
GRID_W = 64
S5_GROUP = 16
S5_GROUPS = D_MODEL // S5_GROUP
S5_STATE = 64
S5_CHUNK = 16
S5_GROUPS_PER_STEP = 4
NA_HEADS = 16
NA_HEAD_DIM = D_MODEL // NA_HEADS
NA_WIN_ROWS = 8
NA_WIN_COLS = 16
NA_QCOLS = 16
NA_KCOLS = 32
NA_ROW_BLOCK = 8
NA_KEY_ROWS = 16
NA_MASKED = -1e30
MEM_HEADS = 4
MEM_HEAD_DIM = D_MODEL // MEM_HEADS
N_EXPERTS = 16
EXPERT_FF = 2048
EC_CAPACITY_FACTOR = 2
ROUTE_TILE = 256
ROUTE_ALIGN = 16
ROUTE_WIN = ROUTE_TILE + ROUTE_ALIGN
ROUTE_SMALL_WIN = 128
DISPATCH_SMALL_WIN = 64
DISPATCH_VMEM_BYTES = 40 * 1024 * 1024
DISPATCH_TOKENS = 2048
GATE_LANES = 128
FFN_TILE = 512
FF_CHUNK = 512
DN_ALPHA = (2.0 * DEPTH) ** 0.25
LN_EPS = 1e-5

ROW_TILE = 512
VMEM_LIMIT = 56 * 1024 * 1024


def _cparams(*sem):
    return pltpu.CompilerParams(dimension_semantics=sem, vmem_limit_bytes=VMEM_LIMIT)


def _layer_norm(v, g, b):
    mu = jnp.mean(v, axis=-1, keepdims=True)
    c = v - mu
    var = jnp.mean(c * c, axis=-1, keepdims=True)
    return c * lax.rsqrt(var + LN_EPS) * g + b


def _matmul_kernel(x_ref, w_ref, o_ref):
    o_ref[...] = jnp.dot(x_ref[...].astype(BF16), w_ref[...],
                         preferred_element_type=F32).astype(o_ref.dtype)


def _matmul(x, w, out_dtype, tile=ROW_TILE):
    n, k = x.shape
    m = w.shape[1]
    tile = min(tile, n)
    return pl.pallas_call(
        _matmul_kernel,
        grid=(n // tile,),
        in_specs=[pl.BlockSpec((tile, k), lambda i: (i, 0)),
                  pl.BlockSpec((k, m), lambda i: (0, 0))],
        out_specs=pl.BlockSpec((tile, m), lambda i: (i, 0)),
        out_shape=jax.ShapeDtypeStruct((n, m), out_dtype),
        compiler_params=_cparams("parallel"),
        name="matmul",
    )(x, w)


def _s5_operators(lam_re, lam_im, log_dt, b_re, b_im, c_re, c_im, d_skip):
    n = S5_CHUNK
    dt = jnp.exp(log_dt.astype(F32))[..., None]
    lr = lam_re.astype(F32) * dt
    li = lam_im.astype(F32) * dt
    k = jnp.arange(n + 1, dtype=F32)
    mag = jnp.exp(lr[..., None] * k)
    pw_re = mag * jnp.cos(li[..., None] * k)
    pw_im = mag * jnp.sin(li[..., None] * k)
    x = pw_re[..., 1] - 1.0
    y = pw_im[..., 1]
    a = lam_re.astype(F32)
    b = lam_im.astype(F32)
    den = a * a + b * b
    q_re = ((x * a + y * b) / den)[..., None]
    q_im = ((y * a - x * b) / den)[..., None]
    bb_re = q_re * b_re.astype(F32) - q_im * b_im.astype(F32)
    bb_im = q_re * b_im.astype(F32) + q_im * b_re.astype(F32)
    ct_re = jnp.swapaxes(c_re.astype(F32), 2, 3)[:, :, :, None, :]
    ct_im = jnp.swapaxes(c_im.astype(F32), 2, 3)[:, :, :, None, :]
    cp_re = ct_re * pw_re[..., None] - ct_im * pw_im[..., None]
    cp_im = ct_re * pw_im[..., None] + ct_im * pw_re[..., None]
    kern = jnp.sum(bb_re[:, :, :, :, None, None] * cp_re[:, :, :, None]
                   - bb_im[:, :, :, :, None, None] * cp_im[:, :, :, None], axis=2)
    pad_t = lambda a, lo, hi: jnp.pad(a, ((0, 0), (0, 0), (lo, hi), (0, 0)))
    t_rows = [pad_t(kern[0][:, :, :n - s], s, 0) + pad_t(jnp.flip(kern[1][:, :, :s + 1], axis=2), 0, n - 1 - s)
              for s in range(n)]
    t_op = jnp.stack(t_rows, axis=1).reshape(S5_GROUPS, n * 16, n * 16)
    bt_re = jnp.swapaxes(bb_re, 2, 3)[:, :, None]
    bt_im = jnp.swapaxes(bb_im, 2, 3)[:, :, None]

    def b_cols(d, pw_r, pw_i):
        pr_ = jnp.swapaxes(pw_r, 1, 2)[:, :, None, :]
        pi_ = jnp.swapaxes(pw_i, 1, 2)[:, :, None, :]
        re = pr_ * bt_re[d] - pi_ * bt_im[d]
        im = pr_ * bt_im[d] + pi_ * bt_re[d]
        return (re.reshape(S5_GROUPS, n * 16, S5_STATE), im.reshape(S5_GROUPS, n * 16, S5_STATE))

    bf_re, bf_im = b_cols(0, jnp.flip(pw_re[0][..., :n], axis=-1), jnp.flip(pw_im[0][..., :n], axis=-1))
    bb_re_, bb_im_ = b_cols(1, pw_re[1][..., :n], pw_im[1][..., :n])
    b_op = jnp.concatenate([bf_re, bb_re_, bf_im, bb_im_], axis=-1)
    c_rows = lambda a: a.reshape(S5_GROUPS, S5_STATE, n * 16)
    c_op = jnp.concatenate([c_rows(cp_re[0][:, :, 1:]), c_rows(jnp.flip(cp_re[1][:, :, 1:], axis=2)),
                            c_rows(-cp_im[0][:, :, 1:]), c_rows(-jnp.flip(cp_im[1][:, :, 1:], axis=2))],
                           axis=1)
    a_re = jnp.concatenate([pw_re[0][..., n], pw_re[1][..., n]], axis=-1)
    a_im = jnp.concatenate([pw_im[0][..., n], pw_im[1][..., n]], axis=-1)
    a_op = jnp.stack([a_re, a_im], axis=1)
    a_op = jnp.broadcast_to(a_op[:, :, None, :], (S5_GROUPS, 2, 8, 128))
    d_op = jnp.tile(d_skip.astype(F32).reshape(S5_GROUPS, 1, 16), (1, 1, n))
    return t_op.astype(BF16), b_op.astype(BF16), c_op.astype(BF16), a_op, d_op


def _s5_kernel(u_ref, t_ref, b_ref, c_ref, a_ref, d_ref, o_ref, s_scr, hc_scr, *, n_tiles):
    gb = u_ref.shape[0]
    for gi in range(gb):
        s_scr[gi] = jnp.dot(u_ref[gi].astype(BF16), b_ref[gi], preferred_element_type=F32)

    row = lax.broadcasted_iota(I32, (8, 128), 0)
    lane = lax.broadcasted_iota(I32, (8, 128), 1)
    low_rows = row < 4
    fwd_lanes = lane < S5_STATE
    a_re = [a_ref[gi, 0] for gi in range(gb)]
    a_im = [a_ref[gi, 1] for gi in range(gb)]

    def cmul_add(gi, h_re, h_im, v_re, v_im):
        return (a_re[gi] * h_re - a_im[gi] * h_im + v_re,
                a_re[gi] * h_im + a_im[gi] * h_re + v_im)

    def half_steps(gi, k, c_re, c_im):
        rows = pl.ds(pl.multiple_of(k * 8, 8), 8)
        v_re = s_scr[gi, rows, 0:128]
        v_im = s_scr[gi, rows, 128:256]
        h1_re, h1_im = cmul_add(gi, c_re, c_im, v_re, v_im)
        h1s_re = pltpu.roll(h1_re, 4, 0)
        h1s_im = pltpu.roll(h1_im, 4, 0)
        h2_re, h2_im = cmul_add(gi, h1s_re, h1s_im, v_re, v_im)
        return rows, h1s_re, h1s_im, pltpu.roll(h2_re, 4, 0), pltpu.roll(h2_im, 4, 0)

    def fwd_body(k, carry):
        out = []
        for gi in range(gb):
            c_re, c_im = carry[2 * gi], carry[2 * gi + 1]
            rows, h1s_re, h1s_im, n_re, n_im = half_steps(gi, k, c_re, c_im)
            hc_scr[gi, rows, 0:128] = jnp.where(low_rows, c_re, h1s_re)
            hc_scr[gi, rows, 128:256] = jnp.where(low_rows, c_im, h1s_im)
            out += [n_re, n_im]
        return tuple(out)

    def bwd_body(i, carry):
        k = n_tiles - 1 - i
        out = []
        for gi in range(gb):
            c_re, c_im = carry[2 * gi], carry[2 * gi + 1]
            rows, h1s_re, h1s_im, n_re, n_im = half_steps(gi, k, c_re, c_im)
            new_re = jnp.where(low_rows, h1s_re, c_re)
            new_im = jnp.where(low_rows, h1s_im, c_im)
            hc_scr[gi, rows, 0:128] = jnp.where(fwd_lanes, hc_scr[gi, rows, 0:128], new_re)
            hc_scr[gi, rows, 128:256] = jnp.where(fwd_lanes, hc_scr[gi, rows, 128:256], new_im)
            out += [n_re, n_im]
        return tuple(out)

    zeros = tuple(jnp.zeros((8, 128), F32) for _ in range(2 * gb))
    lax.fori_loop(0, n_tiles, fwd_body, zeros)
    lax.fori_loop(0, n_tiles, bwd_body, zeros)

    for gi in range(gb):
        u = u_ref[gi]
        y = (jnp.dot(u.astype(BF16), t_ref[gi], preferred_element_type=F32)
             + jnp.dot(hc_scr[gi].astype(BF16), c_ref[gi], preferred_element_type=F32)
             + d_ref[gi] * u)
        o_ref[gi] = jax.nn.gelu(y).astype(o_ref.dtype)


def _s5_scan(u_rows, ops):
    t_op, b_op, c_op, a_op, d_op = ops
    g, m, w = u_rows.shape
    gb = S5_GROUPS_PER_STEP
    spec3 = lambda shape: pl.BlockSpec((gb,) + shape, lambda i: (i,) + (0,) * len(shape))
    return pl.pallas_call(
        functools.partial(_s5_kernel, n_tiles=m // 8),
        grid=(g // gb,),
        in_specs=[spec3((m, w)), spec3((w, w)), spec3((w, w)), spec3((w, w)),
                  spec3((2, 8, 128)), spec3((1, w))],
        out_specs=spec3((m, w)),
        out_shape=jax.ShapeDtypeStruct((g, m, w), F32),
        scratch_shapes=[pltpu.VMEM((gb, m, w), F32), pltpu.VMEM((gb, m, w), F32)],
        compiler_params=_cparams("parallel"),
        name="s5_scan",
    )(u_rows, t_op, b_op, c_op, a_op, d_op)


def _proj_ln_kernel(h_ref, x_ref, w_ref, lg_ref, lb_ref, o_ref):
    h = jnp.dot(h_ref[...], w_ref[...], preferred_element_type=F32)
    o_ref[...] = _layer_norm(DN_ALPHA * x_ref[...] + h, lg_ref[...], lb_ref[...])


def _residual_ln(body, h, x, w, ln_g, ln_b, name):
    n, d = x.shape
    tile = min(ROW_TILE, n)
    row = lambda width: pl.BlockSpec((tile, width), lambda i: (i, 0))
    const = lambda shape: pl.BlockSpec(shape, lambda i: (0, 0))
    return pl.pallas_call(
        body,
        grid=(n // tile,),
        in_specs=[row(h.shape[1]), row(d), const(w.shape), const((1, d)), const((1, d))],
        out_specs=row(d),
        out_shape=jax.ShapeDtypeStruct((n, d), F32),
        compiler_params=_cparams("parallel"),
        name=name,
    )(h, x, w, ln_g.reshape(1, d), ln_b.reshape(1, d))


def _block_transpose(x):
    rows, lanes = x.ndim - 2, x.ndim - 1
    row = lax.broadcasted_iota(I32, x.shape, rows)
    blk = lax.broadcasted_iota(I32, x.shape, lanes) // S5_GROUP
    for d in (1, 2, 4):
        up = pltpu.roll(pltpu.roll(x, 8 - d, rows), S5_GROUP * d, lanes)
        down = pltpu.roll(pltpu.roll(x, d, rows), 128 - S5_GROUP * d, lanes)
        col_bit = (blk & d) != 0
        x = jnp.where((row & d) == (blk & d), x, jnp.where(col_bit, up, down))
    return x


def _s5_in_kernel(x_ref, w_ref, o_ref):
    bsz, steps, d = x_ref.shape
    n_chunk = steps // S5_CHUNK
    u = jnp.dot(x_ref[...].reshape(bsz * steps, d).astype(BF16), w_ref[...], preferred_element_type=F32)
    u = u.reshape(bsz, n_chunk, S5_CHUNK, d)
    for s in range(S5_CHUNK // 8):
        for c in range(d // 128):
            w = _block_transpose(u[:, :, s * 8:(s + 1) * 8, c * 128:(c + 1) * 128])
            for jj in range(n_chunk):
                for b in range(bsz):
                    o_ref[c * 8:(c + 1) * 8, jj * bsz + b, s * 128:(s + 1) * 128] = w[b, jj]


def _s5_glu_ln_kernel(g_ref, x_ref, w_ref, lg_ref, lb_ref, o_ref, tok_scr):
    bsz, steps, d = x_ref.shape
    n_chunk = steps // S5_CHUNK
    for s in range(S5_CHUNK // 8):
        for c in range(d // 128):
            w = jnp.stack([jnp.stack([g_ref[c * 8:(c + 1) * 8, jj * bsz + b, s * 128:(s + 1) * 128]
                                      for jj in range(n_chunk)]) for b in range(bsz)])
            tok_scr[:, :, s * 8:(s + 1) * 8, c * 128:(c + 1) * 128] = _block_transpose(w)
    g = tok_scr[...].reshape(bsz * steps, d).astype(BF16)
    z = jnp.dot(g, w_ref[...], preferred_element_type=F32)
    h = z[:, :d] * jax.nn.sigmoid(z[:, d:])
    x = x_ref[...].reshape(bsz * steps, d)
    o_ref[...] = _layer_norm(DN_ALPHA * x + h, lg_ref[...], lb_ref[...]).reshape(bsz, steps, d)


def _s5_layer(x, w_in, ops, w_glu, ln_g, ln_b):
    bsz, seq, d = x.shape
    j = seq // S5_CHUNK
    steps = ROW_TILE // bsz
    n_chunk = steps // S5_CHUNK
    width = S5_CHUNK * S5_GROUP
    tok_spec = pl.BlockSpec((bsz, steps, d), lambda i: (0, i, 0))
    row_spec = pl.BlockSpec((S5_GROUPS, n_chunk * bsz, width), lambda i: (0, i, 0))
    const = lambda shape: pl.BlockSpec(shape, lambda i: (0, 0))
    u_rows = pl.pallas_call(
        _s5_in_kernel,
        grid=(seq // steps,),
        in_specs=[tok_spec, const((d, d))],
        out_specs=row_spec,
        out_shape=jax.ShapeDtypeStruct((S5_GROUPS, j * bsz, width), F32),
        compiler_params=_cparams("parallel"),
        name="s5_in_proj",
    )(x, w_in)
    g_rows = _s5_scan(u_rows, ops)
    return pl.pallas_call(
        _s5_glu_ln_kernel,
        grid=(seq // steps,),
        in_specs=[row_spec, tok_spec, const((d, 2 * d)), const((1, d)), const((1, d))],
        out_specs=tok_spec,
        out_shape=jax.ShapeDtypeStruct((bsz, seq, d), F32),
        scratch_shapes=[pltpu.VMEM((bsz, n_chunk, S5_CHUNK, d), F32)],
        compiler_params=_cparams("parallel"),
        name="s5_glu_ln",
    )(g_rows, x, w_glu, ln_g.reshape(1, d), ln_b.reshape(1, d))


def _na_key_col_starts():
    n_cb = GRID_W // NA_QCOLS
    return [min(max(n * NA_QCOLS - NA_WIN_COLS // 2, 0), GRID_W - NA_KCOLS) for n in range(n_cb)]


def _na_bias_table(rpb):
    exact = lax.Precision.HIGHEST
    n_cb = GRID_W // NA_QCOLS
    starts = jnp.asarray(_na_key_col_starts(), I32)[:, None, None]
    half = NA_WIN_ROWS // 2
    rl = jnp.arange(NA_ROW_BLOCK)[:, None]
    kl = jnp.arange(NA_KEY_ROWS)[None, :]
    rs_rel = jnp.stack([jnp.maximum(rl - half, 0) + 0 * kl,
                        rl - half + 0 * kl,
                        jnp.minimum(rl - half, 0) + 0 * kl])
    kr_rel = kl - half
    row_ok = (kr_rel >= rs_rel) & (kr_rel < rs_rel + NA_WIN_ROWS)
    dr = jnp.clip(kl - rl + (NA_WIN_ROWS - 1 - half), 0, 2 * NA_WIN_ROWS - 2)
    qc = jnp.arange(n_cb)[:, None, None] * NA_QCOLS + jnp.arange(NA_QCOLS)[None, :, None]
    kc = starts + jnp.arange(NA_KCOLS)[None, None, :]
    ws = jnp.clip(qc - NA_WIN_COLS // 2, 0, GRID_W - NA_WIN_COLS)
    col_ok = (kc >= ws) & (kc < ws + NA_WIN_COLS)
    dc = jnp.clip(kc - qc + NA_WIN_COLS - 1, 0, 2 * NA_WIN_COLS - 2)
    pick_dc = (dc[..., None] == jnp.arange(2 * NA_WIN_COLS - 1)).astype(F32)
    pick_dr = (dr[..., None] == jnp.arange(2 * NA_WIN_ROWS - 1)).astype(F32)
    by_col = jnp.einsum('hrc,nqkc->hrnqk', rpb.astype(F32), pick_dc, precision=exact)
    bias = jnp.einsum('hrnqk,alr->nhaqlk', by_col, pick_dr, precision=exact)
    ok = row_ok[None, :, None, :, None, :, None] & col_ok[:, None, None, None, :, None, :]
    t = jnp.where(ok, bias[:, None], NA_MASKED)
    nq = NA_ROW_BLOCK * NA_QCOLS
    nk = NA_KEY_ROWS * NA_KCOLS
    return t.reshape(n_cb, 3, NA_HEADS // 2, 2 * nq, nk)


def _qkv_kernel(x_ref, w_ref, q_ref, k_ref, v_ref):
    qkv = jnp.dot(x_ref[...].astype(BF16), w_ref[...], preferred_element_type=F32)
    q_ref[...] = (qkv[:, :D_MODEL] * (NA_HEAD_DIM ** -0.5)).astype(BF16)
    k = qkv[:, D_MODEL:2 * D_MODEL]
    v = qkv[:, 2 * D_MODEL:]
    for r in range(NA_ROW_BLOCK):
        for n, start in enumerate(_na_key_col_starts()):
            lo = r * GRID_W + start
            k_ref[r, n] = k[lo:lo + NA_KCOLS].astype(BF16)
            v_ref[r, n] = v[lo:lo + NA_KCOLS].astype(BF16)


def _na_kernel(q_ref, kp_ref, kc_ref, kn_ref, vp_ref, vc_ref, vn_ref, bias_ref, o_ref, k_scr, v_scr):
    nq = NA_ROW_BLOCK * NA_QCOLS
    quarter = 4 * NA_KCOLS
    k_scr[0:quarter] = kp_ref[...].reshape(quarter, D_MODEL)
    k_scr[quarter:3 * quarter] = kc_ref[...].reshape(2 * quarter, D_MODEL)
    k_scr[3 * quarter:] = kn_ref[...].reshape(quarter, D_MODEL)
    v_scr[0:quarter] = vp_ref[...].reshape(quarter, D_MODEL)
    v_scr[quarter:3 * quarter] = vc_ref[...].reshape(2 * quarter, D_MODEL)
    v_scr[3 * quarter:] = vn_ref[...].reshape(quarter, D_MODEL)
    q = q_ref[...].reshape(nq, D_MODEL)
    first_head = lax.broadcasted_iota(I32, (nq, 128), 1) < NA_HEAD_DIM
    zero = jnp.zeros((nq, 128), BF16)
    for hp in range(NA_HEADS // 2):
        lanes = slice(hp * 128, (hp + 1) * 128)
        q2 = q[:, lanes]
        qs = jnp.concatenate([jnp.where(first_head, q2, zero), jnp.where(first_head, zero, q2)], axis=0)
        s = lax.dot_general(qs, k_scr[:, lanes], (((1,), (1,)), ((), ())),
                            preferred_element_type=F32) + bias_ref[hp]
        m = jnp.max(s, axis=-1, keepdims=True)
        p = jnp.exp(s - m)
        l = jnp.sum(p, axis=-1, keepdims=True)
        o = jnp.dot(p.astype(BF16), v_scr[:, lanes], preferred_element_type=F32) / l
        o_ref[:, :, lanes] = jnp.where(first_head, o[:nq], o[nq:]).astype(BF16).reshape(
            NA_ROW_BLOCK, NA_QCOLS, 128)


def _na_layer(x, w_qkv, bias_table, w_o, ln_g, ln_b):
    bsz, seq, d = x.shape
    rows = seq // GRID_W
    n_cb = GRID_W // NA_QCOLS
    n_rb = rows // NA_ROW_BLOCK
    assert n_rb >= 2
    tile = NA_ROW_BLOCK * GRID_W
    q, kx, vx = pl.pallas_call(
        _qkv_kernel,
        grid=(bsz, n_rb),
        in_specs=[pl.BlockSpec((None, tile, d), lambda b, i: (b, i, 0)),
                  pl.BlockSpec((d, 3 * d), lambda b, i: (0, 0))],
        out_specs=[pl.BlockSpec((None, tile, d), lambda b, i: (b, i, 0)),
                   pl.BlockSpec((None, NA_ROW_BLOCK, n_cb, NA_KCOLS, d), lambda b, i: (b, i, 0, 0, 0)),
                   pl.BlockSpec((None, NA_ROW_BLOCK, n_cb, NA_KCOLS, d), lambda b, i: (b, i, 0, 0, 0))],
        out_shape=[jax.ShapeDtypeStruct((bsz, seq, d), BF16),
                   jax.ShapeDtypeStruct((bsz, rows, n_cb, NA_KCOLS, d), BF16),
                   jax.ShapeDtypeStruct((bsz, rows, n_cb, NA_KCOLS, d), BF16)],
        compiler_params=_cparams("parallel", "parallel"),
        name="na_qkv",
    )(x, w_qkv)

    q5 = q.reshape(bsz, rows, n_cb, NA_QCOLS, d)
    n_half = rows // 4
    half_view = lambda a: a.reshape(bsz, n_half, 4, n_cb, NA_KCOLS, d)
    full_view = lambda a: a.reshape(bsz, n_rb, NA_ROW_BLOCK, n_cb, NA_KCOLS, d)
    prev_spec = pl.BlockSpec((None, None, 4, None, NA_KCOLS, d),
                             lambda n, b, i: (b, jnp.maximum(2 * i - 1, 0), 0, n, 0, 0))
    cur_spec = pl.BlockSpec((None, None, NA_ROW_BLOCK, None, NA_KCOLS, d),
                            lambda n, b, i: (b, i, 0, n, 0, 0))
    next_spec = pl.BlockSpec((None, None, 4, None, NA_KCOLS, d),
                             lambda n, b, i: (b, jnp.minimum(2 * i + 2, n_half - 1), 0, n, 0, 0))
    kind = lambda i: jnp.where(i == 0, 0, jnp.where(i == n_rb - 1, 2, 1))
    nk = NA_KEY_ROWS * NA_KCOLS
    o5 = pl.pallas_call(
        _na_kernel,
        grid=(n_cb, bsz, n_rb),
        in_specs=[pl.BlockSpec((None, NA_ROW_BLOCK, None, NA_QCOLS, d), lambda n, b, i: (b, i, n, 0, 0)),
                  prev_spec, cur_spec, next_spec, prev_spec, cur_spec, next_spec,
                  pl.BlockSpec((None, None) + bias_table.shape[2:],
                               lambda n, b, i: (n, kind(i), 0, 0, 0))],
        out_specs=pl.BlockSpec((None, NA_ROW_BLOCK, None, NA_QCOLS, d), lambda n, b, i: (b, i, n, 0, 0)),
        out_shape=jax.ShapeDtypeStruct((bsz, rows, n_cb, NA_QCOLS, d), BF16),
        scratch_shapes=[pltpu.VMEM((nk, d), BF16), pltpu.VMEM((nk, d), BF16)],
        compiler_params=_cparams("parallel", "parallel", "parallel"),
        name="na_attn",
    )(q5, half_view(kx), full_view(kx), half_view(kx), half_view(vx), full_view(vx), half_view(vx),
      bias_table)
    o = o5.reshape(bsz * seq, d)
    return _residual_ln(_proj_ln_kernel, o, x.reshape(bsz * seq, d), w_o, ln_g, ln_b,
                        "na_out_ln").reshape(bsz, seq, d)


def _xattn_kernel(x_ref, k_ref, v_ref, wq_ref, wo_ref, lg_ref, lb_ref, wr_ref, o_ref, ob_ref, aff_ref):
    x = x_ref[...]
    q = jnp.dot(x.astype(BF16), wq_ref[...], preferred_element_type=F32).astype(BF16)
    heads = []
    for h in range(MEM_HEADS):
        lanes = slice(h * MEM_HEAD_DIM, (h + 1) * MEM_HEAD_DIM)
        s = lax.dot_general(q[:, lanes], k_ref[:, lanes], (((1,), (1,)), ((), ())),
                            preferred_element_type=F32) * (MEM_HEAD_DIM ** -0.5)
        m = jnp.max(s, axis=-1, keepdims=True)
        p = jnp.exp(s - m)
        p = p / jnp.sum(p, axis=-1, keepdims=True)
        heads.append(jnp.dot(p.astype(BF16), v_ref[:, lanes], preferred_element_type=F32).astype(BF16))
    o = jnp.concatenate(heads, axis=-1)
    r = jnp.dot(o, wo_ref[...], preferred_element_type=F32)
    y = _layer_norm(DN_ALPHA * x + r, lg_ref[...], lb_ref[...])
    o_ref[...] = y
    yb = y.astype(BF16)
    ob_ref[...] = yb
    logits = lax.dot_general(wr_ref[...], yb, (((1,), (1,)), ((), ())), preferred_element_type=F32)
    e = jnp.exp(logits - jnp.max(logits, axis=0, keepdims=True))
    aff_ref[...] = e / jnp.sum(e, axis=0, keepdims=True)


def _xattn_router(x, mem_k, mem_v, w_q, w_o, ln_g, ln_b, w_router_t):
    bsz, seq, d = x.shape
    m = mem_k.shape[1]
    tile = min(ROW_TILE, seq)
    n_t = seq // tile
    const = lambda shape: pl.BlockSpec(shape, lambda b, i: (0, 0))
    return pl.pallas_call(
        _xattn_kernel,
        grid=(bsz, n_t),
        in_specs=[pl.BlockSpec((None, tile, d), lambda b, i: (b, i, 0)),
                  pl.BlockSpec((None, m, d), lambda b, i: (b, 0, 0)),
                  pl.BlockSpec((None, m, d), lambda b, i: (b, 0, 0)),
                  const((d, d)), const((d, d)), const((1, d)), const((1, d)), const((N_EXPERTS, d))],
        out_specs=[pl.BlockSpec((None, tile, d), lambda b, i: (b, i, 0)),
                   pl.BlockSpec((None, tile, d), lambda b, i: (b, i, 0)),
                   pl.BlockSpec((N_EXPERTS, tile), lambda b, i: (0, b * n_t + i))],
        out_shape=[jax.ShapeDtypeStruct((bsz, seq, d), F32),
                   jax.ShapeDtypeStruct((bsz, seq, d), BF16),
                   jax.ShapeDtypeStruct((N_EXPERTS, bsz * seq), F32)],
        compiler_params=_cparams("parallel", "parallel"),
        name="xattn_router",
    )(x, mem_k, mem_v, w_q, w_o, ln_g.reshape(1, d), ln_b.reshape(1, d), w_router_t)


def _route_kernel(aff_ref, slot_ref, r0_ref, *, cap):
    n = aff_ref.shape[1]
    n_blk = n // ROUTE_TILE
    aff = aff_ref[...]

    def count(mask):
        return jnp.sum(jnp.where(mask, 1.0, 0.0), axis=1, keepdims=True)

    def search(i, bits):
        cand = bits | lax.shift_left(jnp.int32(1), 30 - i)
        ge = aff >= lax.bitcast_convert_type(cand, F32)
        return jnp.where(count(ge) >= cap, cand, bits)

    tau = lax.bitcast_convert_type(
        lax.fori_loop(0, 31, search, jnp.zeros((N_EXPERTS, 1), I32)), F32)
    need = cap - count(aff > tau)
    upper = (lax.broadcasted_iota(I32, (ROUTE_TILE, ROUTE_TILE), 0)
             < lax.broadcasted_iota(I32, (ROUTE_TILE, ROUTE_TILE), 1)).astype(BF16)
    blk_lane = lax.broadcasted_iota(I32, (N_EXPERTS, n_blk), 1)

    r0_ref[...] = jnp.zeros_like(r0_ref)

    def block(kb, carry):
        c_eq, c_sel = carry
        cols = pl.ds(pl.multiple_of(kb * ROUTE_TILE, ROUTE_TILE), ROUTE_TILE)
        b = aff_ref[:, cols]
        eq = b == tau
        eq_f = jnp.where(eq, 1.0, 0.0)
        eq_rank = c_eq + jnp.dot(eq_f.astype(BF16), upper, preferred_element_type=F32)
        sel = (b > tau) | (eq & (eq_rank < need))
        sel_f = jnp.where(sel, 1.0, 0.0)
        rank = c_sel + jnp.dot(sel_f.astype(BF16), upper, preferred_element_type=F32)
        slot_ref[:, cols] = jnp.where(sel, rank.astype(I32), -1)
        r0_ref[...] = jnp.where(blk_lane == kb, c_sel.astype(I32), r0_ref[...])
        return (c_eq + jnp.sum(eq_f, axis=1, keepdims=True),
                c_sel + jnp.sum(sel_f, axis=1, keepdims=True))

    zero = jnp.zeros((N_EXPERTS, 1), F32)
    lax.fori_loop(0, n_blk, block, (zero, zero))


def _route(aff_t, cap):
    e, n = aff_t.shape
    n_blk = n // ROUTE_TILE
    return pl.pallas_call(
        functools.partial(_route_kernel, cap=cap),
        out_shape=[jax.ShapeDtypeStruct((e, n), I32), jax.ShapeDtypeStruct((e, n_blk), I32)],
        compiler_params=pltpu.CompilerParams(vmem_limit_bytes=VMEM_LIMIT),
        name="route",
    )(aff_t)


def _window_start(r0, cap, width):
    start = jnp.minimum((r0 // ROUTE_ALIGN) * ROUTE_ALIGN, cap - width)
    return pl.multiple_of(start, ROUTE_ALIGN)


def _tile_slots(r0_ref, e, blk, n_blk, cap):
    r0 = r0_ref[e, blk]
    nxt = r0_ref[e, jnp.minimum(blk + 1, n_blk - 1)]
    return r0, jnp.where(blk + 1 < n_blk, nxt, cap)


def _fits_small(r0_ref, experts, blk, n_blk, cap, width):
    starts, fits = [], None
    for e in experts:
        r0, r_end = _tile_slots(r0_ref, e, blk, n_blk, cap)
        start = _window_start(r0, cap, width)
        ok = r_end - start <= width
        starts.append(start)
        fits = ok if fits is None else fits & ok
    return starts, fits


def _one_hot(pick):
    return jnp.where(pick, 1.0, 0.0).astype(BF16)


def _split3(g):
    hi = g.astype(BF16).astype(F32)
    mid = (g - hi).astype(BF16).astype(F32)
    return hi, mid, (g - hi - mid).astype(BF16).astype(F32)


def _dispatch_kernel(r0_ref, x_ref, slot_ref, gate_ref, xs_ref, *, cap, n_blk):
    ep = pl.program_id(0)
    j = pl.program_id(1)
    n_sub = x_ref.shape[0] // ROUTE_TILE
    n_exp = xs_ref.shape[0]
    experts = [ep * n_exp + ee for ee in range(n_exp)]

    @pl.when(j == 0)
    def _():
        xs_ref[...] = jnp.zeros_like(xs_ref)

    def scatter(ee, tok, start, width, pick, rows):
        gate = jnp.sum(jnp.where(pick, gate_ref[ee:ee + 1, tok], 0.0), axis=1, keepdims=True)
        hi, mid, lo = _split3(gate)
        lane = lax.broadcasted_iota(I32, (width, GATE_LANES), 1)
        cols = jnp.where(lane == 0, hi, jnp.where(lane == 1, mid, jnp.where(lane == 2, lo, 0.0)))
        win = pl.ds(start, width)
        xs_ref[ee, win, :D_MODEL] = xs_ref[ee, win, :D_MODEL] + rows.astype(BF16)
        xs_ref[ee, win, D_MODEL:] = xs_ref[ee, win, D_MODEL:] + cols.astype(BF16)

    for sb in range(n_sub):
        blk = j * n_sub + sb
        tok = slice(sb * ROUTE_TILE, (sb + 1) * ROUTE_TILE)
        starts, fits = _fits_small(r0_ref, experts, blk, n_blk, cap, DISPATCH_SMALL_WIN)

        @pl.when(fits)
        def _():
            row = lax.broadcasted_iota(I32, (DISPATCH_SMALL_WIN, ROUTE_TILE), 0)
            picks = [row == slot_ref[ee:ee + 1, tok] - starts[ee] for ee in range(n_exp)]
            rows = jnp.dot(jnp.concatenate([_one_hot(p) for p in picks], axis=0), x_ref[tok, :],
                           preferred_element_type=F32)
            for ee in range(n_exp):
                scatter(ee, tok, starts[ee], DISPATCH_SMALL_WIN, picks[ee],
                        rows[ee * DISPATCH_SMALL_WIN:(ee + 1) * DISPATCH_SMALL_WIN])

        @pl.when(jnp.logical_not(fits))
        def _():
            row = lax.broadcasted_iota(I32, (ROUTE_WIN, ROUTE_TILE), 0)
            for ee in range(n_exp):
                start = _window_start(r0_ref[experts[ee], blk], cap, ROUTE_WIN)
                pick = row == slot_ref[ee:ee + 1, tok] - start
                scatter(ee, tok, start, ROUTE_WIN, pick,
                        jnp.dot(_one_hot(pick), x_ref[tok, :], preferred_element_type=F32))


def _dispatch(xb, slot_t, aff_t, r0, cap):
    n, d = xb.shape
    n_blk = n // ROUTE_TILE
    tile = min(DISPATCH_TOKENS, n)
    n_exp = min(N_EXPERTS, DISPATCH_VMEM_BYTES // (2 * cap * (d + GATE_LANES) * 2))
    n_exp = 1 << (n_exp.bit_length() - 1)
    pairs = lambda a: a.reshape(N_EXPERTS // n_exp, n_exp, n)
    pair_spec = pl.BlockSpec((None, n_exp, tile), lambda e, j, r0: (e, 0, j))
    grid_spec = pltpu.PrefetchScalarGridSpec(
        num_scalar_prefetch=1,
        grid=(N_EXPERTS // n_exp, n // tile),
        in_specs=[pl.BlockSpec((tile, d), lambda e, j, r0: (j, 0)), pair_spec, pair_spec],
        out_specs=pl.BlockSpec((n_exp, cap, d + GATE_LANES), lambda e, j, r0: (e, 0, 0)),
    )
    return pl.pallas_call(
        functools.partial(_dispatch_kernel, cap=cap, n_blk=n_blk),
        grid_spec=grid_spec,
        out_shape=jax.ShapeDtypeStruct((N_EXPERTS, cap, d + GATE_LANES), BF16),
        compiler_params=_cparams("parallel", "arbitrary"),
        name="moe_dispatch",
    )(r0, xb, pairs(slot_t), pairs(aff_t))


def _ffn_kernel(xs_ref, w1_ref, w3_ref, w2_ref, o_ref):
    xs = xs_ref[:, :D_MODEL]
    gate = jnp.sum(xs_ref[:, D_MODEL:].astype(F32), axis=1, keepdims=True)
    acc = jnp.zeros((xs.shape[0], D_MODEL), F32)
    for c in range(EXPERT_FF // FF_CHUNK):
        cols = slice(c * FF_CHUNK, (c + 1) * FF_CHUNK)
        h1 = jnp.dot(xs, w1_ref[:, cols].astype(BF16), preferred_element_type=F32)
        h3 = jnp.dot(xs, w3_ref[:, cols].astype(BF16), preferred_element_type=F32)
        h = (jax.nn.silu(h1) * h3).astype(BF16)
        acc = acc + jnp.dot(h, w2_ref[cols, :].astype(BF16), preferred_element_type=F32)
    o_ref[...] = (acc * gate).astype(o_ref.dtype)


def _expert_ffn(xs, w1, w3, w2, layer):
    e, cap, width = xs.shape
    d = D_MODEL
    tile = min(FFN_TILE, cap)
    return pl.pallas_call(
        _ffn_kernel,
        grid=(e, cap // tile),
        in_specs=[pl.BlockSpec((None, tile, width), lambda i, c: (i, c, 0)),
                  pl.BlockSpec((None, None, d, EXPERT_FF), lambda i, c: (layer, i, 0, 0)),
                  pl.BlockSpec((None, None, d, EXPERT_FF), lambda i, c: (layer, i, 0, 0)),
                  pl.BlockSpec((None, None, EXPERT_FF, d), lambda i, c: (layer, i, 0, 0))],
        out_specs=pl.BlockSpec((None, tile, d), lambda i, c: (i, c, 0)),
        out_shape=jax.ShapeDtypeStruct((e, cap, d), BF16),
        compiler_params=_cparams("parallel", "parallel"),
        name="moe_ffn",
    )(xs, w1, w3, w2)


def _combine_kernel(r0_ref, x_ref, slot_ref, lg_ref, lb_ref, ye_ref, o_ref,
                    win_ref, big_ref, acc_ref, sem, big_sem, *, cap):
    j = pl.program_id(0)
    n_blk = pl.num_programs(0)
    buf = j % 2
    experts = range(N_EXPERTS)

    def small_copy(start, b, e):
        return pltpu.make_async_copy(ye_ref.at[e, pl.ds(start, ROUTE_SMALL_WIN)],
                                     win_ref.at[b, pl.ds(e * ROUTE_SMALL_WIN, ROUTE_SMALL_WIN)],
                                     sem.at[b, e])

    starts, fits = _fits_small(r0_ref, experts, j, n_blk, cap, ROUTE_SMALL_WIN)
    nxt = jnp.minimum(j + 1, n_blk - 1)
    nxt_starts, nxt_fits = _fits_small(r0_ref, experts, nxt, n_blk, cap, ROUTE_SMALL_WIN)

    @pl.when((j == 0) & fits)
    def _():
        for e in experts:
            small_copy(starts[e], 0, e).start()

    @pl.when((j + 1 < n_blk) & nxt_fits)
    def _():
        for e in experts:
            small_copy(nxt_starts[e], 1 - buf, e).start()

    slot = slot_ref[...]

    def pick(e, start, width):
        lane = lax.broadcasted_iota(I32, (ROUTE_TILE, width), 1)
        return _one_hot(lane == slot[:, e:e + 1] - start)

    @pl.when(fits)
    def _():
        for e in experts:
            small_copy(starts[e], buf, e).wait()
        picks = jnp.concatenate([pick(e, starts[e], ROUTE_SMALL_WIN) for e in experts], axis=1)
        acc_ref[...] = jnp.dot(picks, win_ref[buf], preferred_element_type=F32)

    @pl.when(jnp.logical_not(fits))
    def _():
        acc = jnp.zeros((ROUTE_TILE, D_MODEL), F32)
        for e in experts:
            start = _window_start(r0_ref[e, j], cap, ROUTE_WIN)
            copy = pltpu.make_async_copy(ye_ref.at[e, pl.ds(start, ROUTE_WIN)], big_ref, big_sem)
            copy.start()
            copy.wait()
            acc = acc + jnp.dot(pick(e, start, ROUTE_WIN), big_ref[...], preferred_element_type=F32)
        acc_ref[...] = acc

    o_ref[...] = _layer_norm(DN_ALPHA * x_ref[...] + acc_ref[...], lg_ref[...], lb_ref[...])


def _combine(x, slot, ye, r0, ln_g, ln_b, cap):
    n, d = x.shape
    n_blk = n // ROUTE_TILE
    grid_spec = pltpu.PrefetchScalarGridSpec(
        num_scalar_prefetch=1,
        grid=(n_blk,),
        in_specs=[pl.BlockSpec((ROUTE_TILE, d), lambda j, r0: (j, 0)),
                  pl.BlockSpec((ROUTE_TILE, N_EXPERTS), lambda j, r0: (j, 0)),
                  pl.BlockSpec((1, d), lambda j, r0: (0, 0)),
                  pl.BlockSpec((1, d), lambda j, r0: (0, 0)),
                  pl.BlockSpec(memory_space=pl.ANY)],
        out_specs=pl.BlockSpec((ROUTE_TILE, d), lambda j, r0: (j, 0)),
        scratch_shapes=[pltpu.VMEM((2, N_EXPERTS * ROUTE_SMALL_WIN, d), BF16),
                        pltpu.VMEM((ROUTE_WIN, d), BF16),
                        pltpu.VMEM((ROUTE_TILE, d), F32),
                        pltpu.SemaphoreType.DMA((2, N_EXPERTS)),
                        pltpu.SemaphoreType.DMA(())],
    )
    return pl.pallas_call(
        functools.partial(_combine_kernel, cap=cap),
        grid_spec=grid_spec,
        out_shape=jax.ShapeDtypeStruct((n, d), F32),
        compiler_params=_cparams("arbitrary"),
        name="moe_combine",
    )(r0, x, slot, ln_g.reshape(1, d), ln_b.reshape(1, d), ye)


def _moe_layer(x, xb, aff_t, w1, w3, w2, layer, ln_g, ln_b):
    bsz, seq, d = x.shape
    n = bsz * seq
    cap = EC_CAPACITY_FACTOR * n // N_EXPERTS
    slot_t, r0 = _route(aff_t, cap)
    xs = _dispatch(xb.reshape(n, d), slot_t, aff_t, r0, cap)
    ye = _expert_ffn(xs, w1, w3, w2, layer)
    out = _combine(x.reshape(n, d), slot_t.T, ye, r0, ln_g, ln_b, cap)
    return out.reshape(bsz, seq, d)


def _trunk(x, mem, p):
    bsz = x.shape[0]
    m = mem.shape[1]
    for i in range(DEPTH):
        if i % 2 == 0:
            x = _s5_layer(x, p["a_w_in"][i // 2], p["s5_ops"][i // 2], p["a_w_glu"][i // 2],
                          p["ln_g"][i, 0], p["ln_b"][i, 0])
        else:
            x = _na_layer(x, p["b_w_qkv"][i // 2], p["na_bias"][i // 2], p["b_w_o"][i // 2],
                          p["ln_g"][i, 0], p["ln_b"][i, 0])
        kv = _matmul(mem.reshape(bsz * m, D_MODEL), p["m_w_kv"][i], BF16).reshape(bsz, m, 2 * D_MODEL)
        x, xb, aff_t = _xattn_router(x, kv[:, :, :D_MODEL], kv[:, :, D_MODEL:], p["m_w_q"][i], p["m_w_o"][i],
                                     p["ln_g"][i, 1], p["ln_b"][i, 1], p["e_w_router_t"][i])
        x = _moe_layer(x, xb, aff_t, p["e_w1"], p["e_w3"], p["e_w2"], i,
                       p["ln_g"][i, 2], p["ln_b"][i, 2])
    return x


def kernel(x_prompt, x_sample, mem_prompt, mem_sample, a_w_in, a_lam_re, a_lam_im, a_log_dt, a_b_re, a_b_im, a_c_re, a_c_im, a_d, a_w_glu, b_w_qkv, b_rpb, b_w_o, m_w_q, m_w_kv, m_w_o, e_w_router, e_w1, e_w3, e_w2, ln_g, ln_b):
    bf = lambda w: w.astype(BF16)
    p = {
        "a_w_in": bf(a_w_in), "a_w_glu": bf(a_w_glu),
        "s5_ops": [_s5_operators(a_lam_re[j], a_lam_im[j], a_log_dt[j], a_b_re[j], a_b_im[j],
                                 a_c_re[j], a_c_im[j], a_d[j]) for j in range(a_w_in.shape[0])],
        "b_w_qkv": bf(b_w_qkv), "b_w_o": bf(b_w_o),
        "na_bias": [_na_bias_table(b_rpb[j]) for j in range(b_rpb.shape[0])],
        "m_w_q": bf(m_w_q), "m_w_kv": bf(m_w_kv), "m_w_o": bf(m_w_o),
        "e_w_router_t": bf(jnp.swapaxes(e_w_router, 1, 2)),
        "e_w1": e_w1, "e_w3": e_w3, "e_w2": e_w2,
        "ln_g": ln_g.astype(F32), "ln_b": ln_b.astype(F32),
    }
    return (_trunk(x_prompt, mem_prompt, p), _trunk(x_sample, mem_sample, p))
```

```python
import functools
import math

import jax
import jax.numpy as jnp
from jax import lax
from jax.experimental import pallas as pl
from jax.experimental.pallas import tpu as pltpu

F32 = jnp.float32
BF16 = jnp.bfloat16
I32 = jnp.int32

D_MODEL = 1024
DEPTH = 2
GRID_W = 64
S5_GROUP = 16
S5_GROUPS = D_MODEL // S5_GROUP
S5_STATE = 64
S5_CHUNK = 16
S5_GROUPS_PER_STEP = 4
NA_HEADS = 16
NA_HEAD_DIM = D_MODEL // NA_HEADS
NA_WIN_ROWS = 8
NA_WIN_COLS = 16
NA_QCOLS = 16
NA_KCOLS = 32
NA_ROW_BLOCK = 8
NA_KEY_ROWS = 16
NA_MASKED = -1e30
MEM_HEADS = 4
MEM_HEAD_DIM = D_MODEL // MEM_HEADS
N_EXPERTS = 16
EXPERT_FF = 2048
EC_CAPACITY_FACTOR = 2
ROUTE_TILE = 256
ROUTE_ALIGN = 16
ROUTE_WIN = ROUTE_TILE + ROUTE_ALIGN
ROUTE_SMALL_WIN = 128
DISPATCH_SMALL_WIN = 64
DISPATCH_VMEM_BYTES = 40 * 1024 * 1024
DISPATCH_TOKENS = 2048
GATE_LANES = 128
FFN_TILE = 512
FF_CHUNK = 512
DN_ALPHA = (2.0 * DEPTH) ** 0.25
LN_EPS = 1e-5

ROW_TILE = 512
XATTN_TILE = 1024
VMEM_LIMIT = 56 * 1024 * 1024


def _cparams(*sem):
    return pltpu.CompilerParams(dimension_semantics=sem, vmem_limit_bytes=VMEM_LIMIT)


def _layer_norm(v, g, b):
    mu = jnp.mean(v, axis=-1, keepdims=True)
    c = v - mu
    var = jnp.mean(c * c, axis=-1, keepdims=True)
    return c * lax.rsqrt(var + LN_EPS) * g + b


def _matmul_kernel(x_ref, w_ref, o_ref):
    o_ref[...] = jnp.dot(x_ref[...].astype(BF16), w_ref[...],
                         preferred_element_type=F32).astype(o_ref.dtype)


def _matmul(x, w, out_dtype, tile=ROW_TILE):
    n, k = x.shape
    m = w.shape[1]
    tile = min(tile, n)
    return pl.pallas_call(
        _matmul_kernel,
        grid=(n // tile,),
        in_specs=[pl.BlockSpec((tile, k), lambda i: (i, 0)),
                  pl.BlockSpec((k, m), lambda i: (0, 0))],
        out_specs=pl.BlockSpec((tile, m), lambda i: (i, 0)),
        out_shape=jax.ShapeDtypeStruct((n, m), out_dtype),
        compiler_params=_cparams("parallel"),
        name="matmul",
    )(x, w)


def _s5_ops_kernel(pw_re_ref, pw_im_ref, pws_re_ref, pws_im_ref, bt_re_ref, bt_im_ref,
                   btb_re_ref, btb_im_ref, ct_re_ref, ct_im_ref, t_ref, b_ref, c_ref):
    n = S5_CHUNK
    w = n * S5_GROUP
    exact = lax.Precision.HIGHEST
    lane = lax.broadcasted_iota(I32, (S5_STATE, w), 1)
    k_blk = lane // S5_GROUP
    c_idx = lane % S5_GROUP
    t_lane = lax.broadcasted_iota(I32, (S5_GROUP, w), 1)

    def spread_k(pw, k_of_lane):
        out = jnp.zeros((S5_STATE, w), F32)
        for k in range(n + 1):
            out = jnp.where(k_of_lane == k, pw[:, k:k + 1], out)
        return out

    def spread_c(ct):
        out = jnp.zeros((S5_STATE, w), F32)
        for c in range(S5_GROUP):
            out = jnp.where(c_idx == c, ct[:, c:c + 1], out)
        return out

    for gi in range(t_ref.shape[0]):
        kern = []
        for d in range(2):
            pw_re, pw_im = pw_re_ref[d, gi], pw_im_ref[d, gi]
            c_re, c_im = spread_c(ct_re_ref[d, gi]), spread_c(ct_im_ref[d, gi])
            lag = k_blk if d == 0 else (n - 1) - k_blk
            out = k_blk + 1 if d == 0 else n - k_blk
            a_re, a_im = spread_k(pw_re, lag), spread_k(pw_im, lag)
            cp_re, cp_im = c_re * a_re - c_im * a_im, c_re * a_im + c_im * a_re
            kern.append(jnp.dot(bt_re_ref[d, gi], cp_re, precision=exact, preferred_element_type=F32)
                        - jnp.dot(bt_im_ref[d, gi], cp_im, precision=exact, preferred_element_type=F32))
            a_re, a_im = spread_k(pw_re, out), spread_k(pw_im, out)
            c_ref[gi, d * S5_STATE:(d + 1) * S5_STATE, :] = (c_re * a_re - c_im * a_im).astype(BF16)
            c_ref[gi, (2 + d) * S5_STATE:(3 + d) * S5_STATE, :] = (-(c_re * a_im + c_im * a_re)).astype(BF16)
        kern_f, kern_b = kern
        bt_re, bt_im = btb_re_ref[gi], btb_im_ref[gi]
        for s in range(n):
            rows = slice(s * S5_GROUP, (s + 1) * S5_GROUP)
            fwd = pltpu.roll(kern_f, s * S5_GROUP, 1) if s else kern_f
            fwd = jnp.where(t_lane >= s * S5_GROUP, fwd, 0.0)
            shift = ((s + 1) * S5_GROUP) % w
            bwd = pltpu.roll(kern_b, shift, 1) if shift else kern_b
            bwd = jnp.where(t_lane < (s + 1) * S5_GROUP, bwd, 0.0)
            t_ref[gi, rows, :] = (fwd + bwd).astype(BF16)
            p_re, p_im = pws_re_ref[gi, s:s + 1, :], pws_im_ref[gi, s:s + 1, :]
            b_ref[gi, rows, :2 * S5_STATE] = (p_re * bt_re - p_im * bt_im).astype(BF16)
            b_ref[gi, rows, 2 * S5_STATE:] = (p_re * bt_im + p_im * bt_re).astype(BF16)


def _s5_operators(lam_re, lam_im, log_dt, b_re, b_im, c_re, c_im, d_skip):
    n = S5_CHUNK
    dt = jnp.exp(log_dt.astype(F32))[..., None]
    lr = lam_re.astype(F32) * dt
    li = lam_im.astype(F32) * dt
    k = jnp.arange(n + 1, dtype=F32)
    mag = jnp.exp(lr[..., None] * k)
    pw_re = mag * jnp.cos(li[..., None] * k)
    pw_im = mag * jnp.sin(li[..., None] * k)
    x = pw_re[..., 1] - 1.0
    y = pw_im[..., 1]
    a = lam_re.astype(F32)
    b = lam_im.astype(F32)
    den = a * a + b * b
    q_re = ((x * a + y * b) / den)[..., None]
    q_im = ((y * a - x * b) / den)[..., None]
    bb_re = q_re * b_re.astype(F32) - q_im * b_im.astype(F32)
    bb_im = q_re * b_im.astype(F32) + q_im * b_re.astype(F32)
    bt_re = jnp.swapaxes(bb_re, 2, 3)
    bt_im = jnp.swapaxes(bb_im, 2, 3)
    ct_re = jnp.swapaxes(c_re.astype(F32), 2, 3)
    ct_im = jnp.swapaxes(c_im.astype(F32), 2, 3)
    both = lambda fwd, bwd: jnp.concatenate([fwd, bwd], axis=-1)
    pwt_re = jnp.swapaxes(pw_re[..., :n], 2, 3)
    pwt_im = jnp.swapaxes(pw_im[..., :n], 2, 3)
    pws_re = both(jnp.flip(pwt_re[0], axis=1), pwt_re[1])
    pws_im = both(jnp.flip(pwt_im[0], axis=1), pwt_im[1])
    gb = S5_GROUPS_PER_STEP
    per_dir = lambda shape: pl.BlockSpec((2, gb) + shape, lambda i: (0, i, 0, 0))
    per_grp = lambda shape: pl.BlockSpec((gb,) + shape, lambda i: (i, 0, 0))
    w = n * S5_GROUP
    t_op, b_op, c_op = pl.pallas_call(
        _s5_ops_kernel,
        grid=(S5_GROUPS // gb,),
        in_specs=[per_dir((S5_STATE, n + 1)), per_dir((S5_STATE, n + 1)),
                  per_grp((n, 2 * S5_STATE)), per_grp((n, 2 * S5_STATE)),
                  per_dir((S5_GROUP, S5_STATE)), per_dir((S5_GROUP, S5_STATE)),
                  per_grp((S5_GROUP, 2 * S5_STATE)), per_grp((S5_GROUP, 2 * S5_STATE)),
                  per_dir((S5_STATE, S5_GROUP)), per_dir((S5_STATE, S5_GROUP))],
        out_specs=[per_grp((w, w))] * 3,
        out_shape=[jax.ShapeDtypeStruct((S5_GROUPS, w, w), BF16)] * 3,
        compiler_params=_cparams("parallel"),
        name="s5_operators",
    )(pw_re, pw_im, pws_re, pws_im, bt_re, bt_im, both(bt_re[0], bt_re[1]), both(bt_im[0], bt_im[1]),
      ct_re, ct_im)
    a_re = jnp.concatenate([pw_re[0][..., n], pw_re[1][..., n]], axis=-1)
    a_im = jnp.concatenate([pw_im[0][..., n], pw_im[1][..., n]], axis=-1)
    a_op = jnp.stack([a_re, a_im], axis=1)
    a_op = jnp.broadcast_to(a_op[:, :, None, :], (S5_GROUPS, 2, 8, 128))
    d_op = jnp.tile(d_skip.astype(F32).reshape(S5_GROUPS, 1, 16), (1, 1, n))
    return t_op, b_op, c_op, a_op, d_op


def _s5_kernel(u_ref, t_ref, b_ref, c_ref, a_ref, d_ref, o_ref, s_scr, hc_scr, *, n_tiles):
    gb = u_ref.shape[0]
    for gi in range(gb):
        s_scr[gi] = jnp.dot(u_ref[gi].astype(BF16), b_ref[gi], preferred_element_type=F32)

    row = lax.broadcasted_iota(I32, (8, 128), 0)
    lane = lax.broadcasted_iota(I32, (8, 128), 1)
    low_rows = row < 4
    fwd_lanes = lane < S5_STATE
    a_re = [a_ref[gi, 0] for gi in range(gb)]
    a_im = [a_ref[gi, 1] for gi in range(gb)]

    def cmul_add(gi, h_re, h_im, v_re, v_im):
        return (a_re[gi] * h_re - a_im[gi] * h_im + v_re,
                a_re[gi] * h_im + a_im[gi] * h_re + v_im)

    def half_steps(gi, k, c_re, c_im):
        rows = pl.ds(pl.multiple_of(k * 8, 8), 8)
        v_re = s_scr[gi, rows, 0:128]
        v_im = s_scr[gi, rows, 128:256]
        h1_re, h1_im = cmul_add(gi, c_re, c_im, v_re, v_im)
        h1s_re = pltpu.roll(h1_re, 4, 0)
        h1s_im = pltpu.roll(h1_im, 4, 0)
        h2_re, h2_im = cmul_add(gi, h1s_re, h1s_im, v_re, v_im)
        return rows, h1s_re, h1s_im, pltpu.roll(h2_re, 4, 0), pltpu.roll(h2_im, 4, 0)

    def fwd_body(k, carry):
        out = []
        for gi in range(gb):
            c_re, c_im = carry[2 * gi], carry[2 * gi + 1]
            rows, h1s_re, h1s_im, n_re, n_im = half_steps(gi, k, c_re, c_im)
            hc_scr[gi, rows, 0:128] = jnp.where(low_rows, c_re, h1s_re)
            hc_scr[gi, rows, 128:256] = jnp.where(low_rows, c_im, h1s_im)
            out += [n_re, n_im]
        return tuple(out)

    def bwd_body(i, carry):
        k = n_tiles - 1 - i
        out = []
        for gi in range(gb):
            c_re, c_im = carry[2 * gi], carry[2 * gi + 1]
            rows, h1s_re, h1s_im, n_re, n_im = half_steps(gi, k, c_re, c_im)
            new_re = jnp.where(low_rows, h1s_re, c_re)
            new_im = jnp.where(low_rows, h1s_im, c_im)
            hc_scr[gi, rows, 0:128] = jnp.where(fwd_lanes, hc_scr[gi, rows, 0:128], new_re)
            hc_scr[gi, rows, 128:256] = jnp.where(fwd_lanes, hc_scr[gi, rows, 128:256], new_im)
            out += [n_re, n_im]
        return tuple(out)

    zeros = tuple(jnp.zeros((8, 128), F32) for _ in range(2 * gb))
    lax.fori_loop(0, n_tiles, fwd_body, zeros)
    lax.fori_loop(0, n_tiles, bwd_body, zeros)

    for gi in range(gb):
        u = u_ref[gi]
        y = (jnp.dot(u.astype(BF16), t_ref[gi], preferred_element_type=F32)
             + jnp.dot(hc_scr[gi].astype(BF16), c_ref[gi], preferred_element_type=F32)
             + d_ref[gi] * u)
        o_ref[gi] = jax.nn.gelu(y).astype(o_ref.dtype)


def _s5_scan(u_rows, ops):
    t_op, b_op, c_op, a_op, d_op = ops
    g, m, w = u_rows.shape
    gb = S5_GROUPS_PER_STEP
    spec3 = lambda shape: pl.BlockSpec((gb,) + shape, lambda i: (i,) + (0,) * len(shape))
    return pl.pallas_call(
        functools.partial(_s5_kernel, n_tiles=m // 8),
        grid=(g // gb,),
        in_specs=[spec3((m, w)), spec3((w, w)), spec3((w, w)), spec3((w, w)),
                  spec3((2, 8, 128)), spec3((1, w))],
        out_specs=spec3((m, w)),
        out_shape=jax.ShapeDtypeStruct((g, m, w), F32),
        scratch_shapes=[pltpu.VMEM((gb, m, w), F32), pltpu.VMEM((gb, m, w), F32)],
        compiler_params=_cparams("parallel"),
        name="s5_scan",
    )(u_rows, t_op, b_op, c_op, a_op, d_op)


def _proj_ln_kernel(h_ref, x_ref, w_ref, lg_ref, lb_ref, o_ref):
    h = jnp.dot(h_ref[...], w_ref[...], preferred_element_type=F32)
    o_ref[...] = _layer_norm(DN_ALPHA * x_ref[...] + h, lg_ref[...], lb_ref[...])


def _residual_ln(body, h, x, w, ln_g, ln_b, name):
    n, d = x.shape
    tile = min(ROW_TILE, n)
    row = lambda width: pl.BlockSpec((tile, width), lambda i: (i, 0))
    const = lambda shape: pl.BlockSpec(shape, lambda i: (0, 0))
    return pl.pallas_call(
        body,
        grid=(n // tile,),
        in_specs=[row(h.shape[1]), row(d), const(w.shape), const((1, d)), const((1, d))],
        out_specs=row(d),
        out_shape=jax.ShapeDtypeStruct((n, d), F32),
        compiler_params=_cparams("parallel"),
        name=name,
    )(h, x, w, ln_g.reshape(1, d), ln_b.reshape(1, d))


def _block_transpose(x):
    rows, lanes = x.ndim - 2, x.ndim - 1
    row = lax.broadcasted_iota(I32, x.shape, rows)
    blk = lax.broadcasted_iota(I32, x.shape, lanes) // S5_GROUP
    for d in (1, 2, 4):
        up = pltpu.roll(pltpu.roll(x, 8 - d, rows), S5_GROUP * d, lanes)
        down = pltpu.roll(pltpu.roll(x, d, rows), 128 - S5_GROUP * d, lanes)
        col_bit = (blk & d) != 0
        x = jnp.where((row & d) == (blk & d), x, jnp.where(col_bit, up, down))
    return x


def _s5_in_kernel(x_ref, w_ref, o_ref):
    bsz, steps, d = x_ref.shape
    n_chunk = steps // S5_CHUNK
    u = jnp.dot(x_ref[...].reshape(bsz * steps, d).astype(BF16), w_ref[...], preferred_element_type=F32)
    u = u.reshape(bsz, n_chunk, S5_CHUNK, d)
    for s in range(S5_CHUNK // 8):
        for c in range(d // 128):
            w = _block_transpose(u[:, :, s * 8:(s + 1) * 8, c * 128:(c + 1) * 128])
            for jj in range(n_chunk):
                for b in range(bsz):
                    o_ref[c * 8:(c + 1) * 8, jj * bsz + b, s * 128:(s + 1) * 128] = w[b, jj]


def _s5_glu_ln_kernel(g_ref, x_ref, w_ref, lg_ref, lb_ref, o_ref, tok_scr):
    bsz, steps, d = x_ref.shape
    n_chunk = steps // S5_CHUNK
    for s in range(S5_CHUNK // 8):
        for c in range(d // 128):
            w = jnp.stack([jnp.stack([g_ref[c * 8:(c + 1) * 8, jj * bsz + b, s * 128:(s + 1) * 128]
                                      for jj in range(n_chunk)]) for b in range(bsz)])
            tok_scr[:, :, s * 8:(s + 1) * 8, c * 128:(c + 1) * 128] = _block_transpose(w)
    g = tok_scr[...].reshape(bsz * steps, d).astype(BF16)
    z = jnp.dot(g, w_ref[...], preferred_element_type=F32)
    h = z[:, :d] * jax.nn.sigmoid(z[:, d:])
    x = x_ref[...].reshape(bsz * steps, d)
    o_ref[...] = _layer_norm(DN_ALPHA * x + h, lg_ref[...], lb_ref[...]).reshape(bsz, steps, d)


def _s5_layer(x, w_in, ops, w_glu, ln_g, ln_b):
    bsz, seq, d = x.shape
    j = seq // S5_CHUNK
    steps = ROW_TILE // bsz
    n_chunk = steps // S5_CHUNK
    width = S5_CHUNK * S5_GROUP
    tok_spec = pl.BlockSpec((bsz, steps, d), lambda i: (0, i, 0))
    row_spec = pl.BlockSpec((S5_GROUPS, n_chunk * bsz, width), lambda i: (0, i, 0))
    const = lambda shape: pl.BlockSpec(shape, lambda i: (0, 0))
    u_rows = pl.pallas_call(
        _s5_in_kernel,
        grid=(seq // steps,),
        in_specs=[tok_spec, const((d, d))],
        out_specs=row_spec,
        out_shape=jax.ShapeDtypeStruct((S5_GROUPS, j * bsz, width), F32),
        compiler_params=_cparams("parallel"),
        name="s5_in_proj",
    )(x, w_in)
    g_rows = _s5_scan(u_rows, ops)
    return pl.pallas_call(
        _s5_glu_ln_kernel,
        grid=(seq // steps,),
        in_specs=[row_spec, tok_spec, const((d, 2 * d)), const((1, d)), const((1, d))],
        out_specs=tok_spec,
        out_shape=jax.ShapeDtypeStruct((bsz, seq, d), F32),
        scratch_shapes=[pltpu.VMEM((bsz, n_chunk, S5_CHUNK, d), F32)],
        compiler_params=_cparams("parallel"),
        name="s5_glu_ln",
    )(g_rows, x, w_glu, ln_g.reshape(1, d), ln_b.reshape(1, d))


def _na_key_col_starts():
    n_cb = GRID_W // NA_QCOLS
    return [min(max(n * NA_QCOLS - NA_WIN_COLS // 2, 0), GRID_W - NA_KCOLS) for n in range(n_cb)]


def _na_bias_table(rpb):
    exact = lax.Precision.HIGHEST
    n_cb = GRID_W // NA_QCOLS
    starts = jnp.asarray(_na_key_col_starts(), I32)[:, None, None]
    half = NA_WIN_ROWS // 2
    rl = jnp.arange(NA_ROW_BLOCK)[:, None]
    kl = jnp.arange(NA_KEY_ROWS)[None, :]
    rs_rel = jnp.stack([jnp.maximum(rl - half, 0) + 0 * kl,
                        rl - half + 0 * kl,
                        jnp.minimum(rl - half, 0) + 0 * kl])
    kr_rel = kl - half
    row_ok = (kr_rel >= rs_rel) & (kr_rel < rs_rel + NA_WIN_ROWS)
    dr = jnp.clip(kl - rl + (NA_WIN_ROWS - 1 - half), 0, 2 * NA_WIN_ROWS - 2)
    qc = jnp.arange(n_cb)[:, None, None] * NA_QCOLS + jnp.arange(NA_QCOLS)[None, :, None]
    kc = starts + jnp.arange(NA_KCOLS)[None, None, :]
    ws = jnp.clip(qc - NA_WIN_COLS // 2, 0, GRID_W - NA_WIN_COLS)
    col_ok = (kc >= ws) & (kc < ws + NA_WIN_COLS)
    dc = jnp.clip(kc - qc + NA_WIN_COLS - 1, 0, 2 * NA_WIN_COLS - 2)
    pick_dc = (dc[..., None] == jnp.arange(2 * NA_WIN_COLS - 1)).astype(F32)
    pick_dr = (dr[..., None] == jnp.arange(2 * NA_WIN_ROWS - 1)).astype(F32)
    by_col = jnp.einsum('hrc,nqkc->hrnqk', rpb.astype(F32), pick_dc, precision=exact)
    bias = jnp.einsum('hrnqk,alr->nhaqlk', by_col, pick_dr, precision=exact)
    ok = row_ok[None, :, None, :, None, :, None] & col_ok[:, None, None, None, :, None, :]
    t = jnp.where(ok, bias[:, None], NA_MASKED)
    nq = NA_ROW_BLOCK * NA_QCOLS
    nk = NA_KEY_ROWS * NA_KCOLS
    return t.reshape(n_cb, 3, NA_HEADS // 2, 2 * nq, nk)


def _qkv_kernel(x_ref, w_ref, q_ref, k_ref, v_ref):
    qkv = jnp.dot(x_ref[...].astype(BF16), w_ref[...], preferred_element_type=F32)
    q_ref[...] = (qkv[:, :D_MODEL] * (NA_HEAD_DIM ** -0.5)).astype(BF16)
    k = qkv[:, D_MODEL:2 * D_MODEL]
    v = qkv[:, 2 * D_MODEL:]
    for r in range(NA_ROW_BLOCK):
        for n, start in enumerate(_na_key_col_starts()):
            lo = r * GRID_W + start
            k_ref[r, n] = k[lo:lo + NA_KCOLS].astype(BF16)
            v_ref[r, n] = v[lo:lo + NA_KCOLS].astype(BF16)


def _na_kernel(q_ref, kp_ref, kc_ref, kn_ref, vp_ref, vc_ref, vn_ref, bias_ref, o_ref, k_scr, v_scr):
    nq = NA_ROW_BLOCK * NA_QCOLS
    quarter = 4 * NA_KCOLS
    k_scr[0:quarter] = kp_ref[...].reshape(quarter, D_MODEL)
    k_scr[quarter:3 * quarter] = kc_ref[...].reshape(2 * quarter, D_MODEL)
    k_scr[3 * quarter:] = kn_ref[...].reshape(quarter, D_MODEL)
    v_scr[0:quarter] = vp_ref[...].reshape(quarter, D_MODEL)
    v_scr[quarter:3 * quarter] = vc_ref[...].reshape(2 * quarter, D_MODEL)
    v_scr[3 * quarter:] = vn_ref[...].reshape(quarter, D_MODEL)
    q = q_ref[...].reshape(nq, D_MODEL)
    first_head = lax.broadcasted_iota(I32, (nq, 128), 1) < NA_HEAD_DIM
    zero = jnp.zeros((nq, 128), BF16)
    for hp in range(NA_HEADS // 2):
        lanes = slice(hp * 128, (hp + 1) * 128)
        q2 = q[:, lanes]
        qs = jnp.concatenate([jnp.where(first_head, q2, zero), jnp.where(first_head, zero, q2)], axis=0)
        s = lax.dot_general(qs, k_scr[:, lanes], (((1,), (1,)), ((), ())),
                            preferred_element_type=F32) + bias_ref[hp]
        m = jnp.max(s, axis=-1, keepdims=True)
        p = jnp.exp(s - m)
        l = jnp.sum(p, axis=-1, keepdims=True)
        o = jnp.dot(p.astype(BF16), v_scr[:, lanes], preferred_element_type=F32) / l
        o_ref[:, :, lanes] = jnp.where(first_head, o[:nq], o[nq:]).astype(BF16).reshape(
            NA_ROW_BLOCK, NA_QCOLS, 128)


def _na_layer(x, w_qkv, bias_table, w_o, ln_g, ln_b):
    bsz, seq, d = x.shape
    rows = seq // GRID_W
    n_cb = GRID_W // NA_QCOLS
    n_rb = rows // NA_ROW_BLOCK
    assert n_rb >= 2
    tile = NA_ROW_BLOCK * GRID_W
    q, kx, vx = pl.pallas_call(
        _qkv_kernel,
        grid=(bsz, n_rb),
        in_specs=[pl.BlockSpec((None, tile, d), lambda b, i: (b, i, 0)),
                  pl.BlockSpec((d, 3 * d), lambda b, i: (0, 0))],
        out_specs=[pl.BlockSpec((None, tile, d), lambda b, i: (b, i, 0)),
                   pl.BlockSpec((None, NA_ROW_BLOCK, n_cb, NA_KCOLS, d), lambda b, i: (b, i, 0, 0, 0)),
                   pl.BlockSpec((None, NA_ROW_BLOCK, n_cb, NA_KCOLS, d), lambda b, i: (b, i, 0, 0, 0))],
        out_shape=[jax.ShapeDtypeStruct((bsz, seq, d), BF16),
                   jax.ShapeDtypeStruct((bsz, rows, n_cb, NA_KCOLS, d), BF16),
                   jax.ShapeDtypeStruct((bsz, rows, n_cb, NA_KCOLS, d), BF16)],
        compiler_params=_cparams("parallel", "parallel"),
        name="na_qkv",
    )(x, w_qkv)

    q5 = q.reshape(bsz, rows, n_cb, NA_QCOLS, d)
    n_half = rows // 4
    half_view = lambda a: a.reshape(bsz, n_half, 4, n_cb, NA_KCOLS, d)
    full_view = lambda a: a.reshape(bsz, n_rb, NA_ROW_BLOCK, n_cb, NA_KCOLS, d)
    prev_spec = pl.BlockSpec((None, None, 4, None, NA_KCOLS, d),
                             lambda n, b, i: (b, jnp.maximum(2 * i - 1, 0), 0, n, 0, 0))
    cur_spec = pl.BlockSpec((None, None, NA_ROW_BLOCK, None, NA_KCOLS, d),
                            lambda n, b, i: (b, i, 0, n, 0, 0))
    next_spec = pl.BlockSpec((None, None, 4, None, NA_KCOLS, d),
                             lambda n, b, i: (b, jnp.minimum(2 * i + 2, n_half - 1), 0, n, 0, 0))
    kind = lambda i: jnp.where(i == 0, 0, jnp.where(i == n_rb - 1, 2, 1))
    nk = NA_KEY_ROWS * NA_KCOLS
    o5 = pl.pallas_call(
        _na_kernel,
        grid=(n_cb, bsz, n_rb),
        in_specs=[pl.BlockSpec((None, NA_ROW_BLOCK, None, NA_QCOLS, d), lambda n, b, i: (b, i, n, 0, 0)),
                  prev_spec, cur_spec, next_spec, prev_spec, cur_spec, next_spec,
                  pl.BlockSpec((None, None) + bias_table.shape[2:],
                               lambda n, b, i: (n, kind(i), 0, 0, 0))],
        out_specs=pl.BlockSpec((None, NA_ROW_BLOCK, None, NA_QCOLS, d), lambda n, b, i: (b, i, n, 0, 0)),
        out_shape=jax.ShapeDtypeStruct((bsz, rows, n_cb, NA_QCOLS, d), BF16),
        scratch_shapes=[pltpu.VMEM((nk, d), BF16), pltpu.VMEM((nk, d), BF16)],
        compiler_params=_cparams("parallel", "parallel", "parallel"),
        name="na_attn",
    )(q5, half_view(kx), full_view(kx), half_view(kx), half_view(vx), full_view(vx), half_view(vx),
      bias_table)
    o = o5.reshape(bsz * seq, d)
    return _residual_ln(_proj_ln_kernel, o, x.reshape(bsz * seq, d), w_o, ln_g, ln_b,
                        "na_out_ln").reshape(bsz, seq, d)


def _xattn_kernel(x_ref, k_ref, v_ref, wq_ref, wo_ref, lg_ref, lb_ref, wr_ref, o_ref, ob_ref, aff_ref):
    x = x_ref[...]
    q = jnp.dot(x.astype(BF16), wq_ref[...], preferred_element_type=F32).astype(BF16)
    heads = []
    for h in range(MEM_HEADS):
        lanes = slice(h * MEM_HEAD_DIM, (h + 1) * MEM_HEAD_DIM)
        s = lax.dot_general(q[:, lanes], k_ref[:, lanes], (((1,), (1,)), ((), ())),
                            preferred_element_type=F32) * (MEM_HEAD_DIM ** -0.5)
        m = jnp.max(s, axis=-1, keepdims=True)
        p = jnp.exp(s - m)
        p = p / jnp.sum(p, axis=-1, keepdims=True)
        heads.append(jnp.dot(p.astype(BF16), v_ref[:, lanes], preferred_element_type=F32).astype(BF16))
    o = jnp.concatenate(heads, axis=-1)
    r = jnp.dot(o, wo_ref[...], preferred_element_type=F32)
    y = _layer_norm(DN_ALPHA * x + r, lg_ref[...], lb_ref[...])
    o_ref[...] = y
    yb = y.astype(BF16)
    ob_ref[...] = yb
    logits = lax.dot_general(wr_ref[...], yb, (((1,), (1,)), ((), ())), preferred_element_type=F32)
    e = jnp.exp(logits - jnp.max(logits, axis=0, keepdims=True))
    aff_ref[...] = e / jnp.sum(e, axis=0, keepdims=True)


def _xattn_router(x, mem_k, mem_v, w_q, w_o, ln_g, ln_b, w_router_t):
    bsz, seq, d = x.shape
    m = mem_k.shape[1]
    tile = min(XATTN_TILE, seq)
    n_t = seq // tile
    const = lambda shape: pl.BlockSpec(shape, lambda b, i: (0, 0))
    return pl.pallas_call(
        _xattn_kernel,
        grid=(bsz, n_t),
        in_specs=[pl.BlockSpec((None, tile, d), lambda b, i: (b, i, 0)),
                  pl.BlockSpec((None, m, d), lambda b, i: (b, 0, 0)),
                  pl.BlockSpec((None, m, d), lambda b, i: (b, 0, 0)),
                  const((d, d)), const((d, d)), const((1, d)), const((1, d)), const((N_EXPERTS, d))],
        out_specs=[pl.BlockSpec((None, tile, d), lambda b, i: (b, i, 0)),
                   pl.BlockSpec((None, tile, d), lambda b, i: (b, i, 0)),
                   pl.BlockSpec((N_EXPERTS, tile), lambda b, i: (0, b * n_t + i))],
        out_shape=[jax.ShapeDtypeStruct((bsz, seq, d), F32),
                   jax.ShapeDtypeStruct((bsz, seq, d), BF16),
                   jax.ShapeDtypeStruct((N_EXPERTS, bsz * seq), F32)],
        compiler_params=_cparams("parallel", "parallel"),
        name="xattn_router",
    )(x, mem_k, mem_v, w_q, w_o, ln_g.reshape(1, d), ln_b.reshape(1, d), w_router_t)


def _route_kernel(aff_ref, slot_ref, r0_ref, *, cap):
    n = aff_ref.shape[1]
    n_blk = n // ROUTE_TILE
    aff = aff_ref[...]

    def count(mask):
        return jnp.sum(jnp.where(mask, 1.0, 0.0), axis=1, keepdims=True)

    def search(i, bits):
        cand = bits | lax.shift_left(jnp.int32(1), 30 - i)
        ge = aff >= lax.bitcast_convert_type(cand, F32)
        return jnp.where(count(ge) >= cap, cand, bits)

    tau = lax.bitcast_convert_type(
        lax.fori_loop(0, 31, search, jnp.zeros((N_EXPERTS, 1), I32)), F32)
    need = cap - count(aff > tau)
    upper = (lax.broadcasted_iota(I32, (ROUTE_TILE, ROUTE_TILE), 0)
             < lax.broadcasted_iota(I32, (ROUTE_TILE, ROUTE_TILE), 1)).astype(BF16)
    blk_lane = lax.broadcasted_iota(I32, (N_EXPERTS, n_blk), 1)

    r0_ref[...] = jnp.zeros_like(r0_ref)

    def block(kb, carry):
        c_eq, c_sel = carry
        cols = pl.ds(pl.multiple_of(kb * ROUTE_TILE, ROUTE_TILE), ROUTE_TILE)
        b = aff_ref[:, cols]
        eq = b == tau
        eq_f = jnp.where(eq, 1.0, 0.0)
        eq_rank = c_eq + jnp.dot(eq_f.astype(BF16), upper, preferred_element_type=F32)
        sel = (b > tau) | (eq & (eq_rank < need))
        sel_f = jnp.where(sel, 1.0, 0.0)
        rank = c_sel + jnp.dot(sel_f.astype(BF16), upper, preferred_element_type=F32)
        slot_ref[:, cols] = jnp.where(sel, rank.astype(I32), -1)
        r0_ref[...] = jnp.where(blk_lane == kb, c_sel.astype(I32), r0_ref[...])
        return (c_eq + jnp.sum(eq_f, axis=1, keepdims=True),
                c_sel + jnp.sum(sel_f, axis=1, keepdims=True))

    zero = jnp.zeros((N_EXPERTS, 1), F32)
    lax.fori_loop(0, n_blk, block, (zero, zero))


def _route(aff_t, cap):
    e, n = aff_t.shape
    n_blk = n // ROUTE_TILE
    return pl.pallas_call(
        functools.partial(_route_kernel, cap=cap),
        out_shape=[jax.ShapeDtypeStruct((e, n), I32), jax.ShapeDtypeStruct((e, n_blk), I32)],
        compiler_params=pltpu.CompilerParams(vmem_limit_bytes=VMEM_LIMIT),
        name="route",
    )(aff_t)


def _window_start(r0, cap, width):
    start = jnp.minimum((r0 // ROUTE_ALIGN) * ROUTE_ALIGN, cap - width)
    return pl.multiple_of(start, ROUTE_ALIGN)


def _tile_slots(r0_ref, e, blk, n_blk, cap):
    r0 = r0_ref[e, blk]
    nxt = r0_ref[e, jnp.minimum(blk + 1, n_blk - 1)]
    return r0, jnp.where(blk + 1 < n_blk, nxt, cap)


def _fits_small(r0_ref, experts, blk, n_blk, cap, width):
    starts, fits = [], None
    for e in experts:
        r0, r_end = _tile_slots(r0_ref, e, blk, n_blk, cap)
        start = _window_start(r0, cap, width)
        ok = r_end - start <= width
        starts.append(start)
        fits = ok if fits is None else fits & ok
    return starts, fits


def _one_hot(pick):
    return jnp.where(pick, 1.0, 0.0).astype(BF16)


def _split3(g):
    hi = g.astype(BF16).astype(F32)
    mid = (g - hi).astype(BF16).astype(F32)
    return hi, mid, (g - hi - mid).astype(BF16).astype(F32)


def _dispatch_kernel(r0_ref, x_ref, slot_ref, gate_ref, xs_ref, *, cap, n_blk):
    ep = pl.program_id(0)
    j = pl.program_id(1)
    n_sub = x_ref.shape[0] // ROUTE_TILE
    n_exp = xs_ref.shape[0]
    experts = [ep * n_exp + ee for ee in range(n_exp)]

    @pl.when(j == 0)
    def _():
        xs_ref[...] = jnp.zeros_like(xs_ref)

    def scatter(ee, tok, start, width, pick, rows):
        gate = jnp.sum(jnp.where(pick, gate_ref[ee:ee + 1, tok], 0.0), axis=1, keepdims=True)
        hi, mid, lo = _split3(gate)
        lane = lax.broadcasted_iota(I32, (width, GATE_LANES), 1)
        cols = jnp.where(lane == 0, hi, jnp.where(lane == 1, mid, jnp.where(lane == 2, lo, 0.0)))
        win = pl.ds(start, width)
        xs_ref[ee, win, :D_MODEL] = xs_ref[ee, win, :D_MODEL] + rows.astype(BF16)
        xs_ref[ee, win, D_MODEL:] = xs_ref[ee, win, D_MODEL:] + cols.astype(BF16)

    for sb in range(n_sub):
        blk = j * n_sub + sb
        tok = slice(sb * ROUTE_TILE, (sb + 1) * ROUTE_TILE)
        starts, fits = _fits_small(r0_ref, experts, blk, n_blk, cap, DISPATCH_SMALL_WIN)

        @pl.when(fits)
        def _():
            row = lax.broadcasted_iota(I32, (DISPATCH_SMALL_WIN, ROUTE_TILE), 0)
            picks = [row == slot_ref[ee:ee + 1, tok] - starts[ee] for ee in range(n_exp)]
            rows = jnp.dot(jnp.concatenate([_one_hot(p) for p in picks], axis=0), x_ref[tok, :],
                           preferred_element_type=F32)
            for ee in range(n_exp):
                scatter(ee, tok, starts[ee], DISPATCH_SMALL_WIN, picks[ee],
                        rows[ee * DISPATCH_SMALL_WIN:(ee + 1) * DISPATCH_SMALL_WIN])

        @pl.when(jnp.logical_not(fits))
        def _():
            row = lax.broadcasted_iota(I32, (ROUTE_WIN, ROUTE_TILE), 0)
            for ee in range(n_exp):
                start = _window_start(r0_ref[experts[ee], blk], cap, ROUTE_WIN)
                pick = row == slot_ref[ee:ee + 1, tok] - start
                scatter(ee, tok, start, ROUTE_WIN, pick,
                        jnp.dot(_one_hot(pick), x_ref[tok, :], preferred_element_type=F32))


def _dispatch(xb, slot_t, aff_t, r0, cap):
    n, d = xb.shape
    n_blk = n // ROUTE_TILE
    tile = min(DISPATCH_TOKENS, n)
    n_exp = min(N_EXPERTS, DISPATCH_VMEM_BYTES // (cap * (d + GATE_LANES) * 2))
    n_exp = 1 << (n_exp.bit_length() - 1)
    pairs = lambda a: a.reshape(N_EXPERTS // n_exp, n_exp, n)
    pair_spec = pl.BlockSpec((None, n_exp, tile), lambda e, j, r0: (e, 0, j))
    grid_spec = pltpu.PrefetchScalarGridSpec(
        num_scalar_prefetch=1,
        grid=(N_EXPERTS // n_exp, n // tile),
        in_specs=[pl.BlockSpec((tile, d), lambda e, j, r0: (j, 0)), pair_spec, pair_spec],
        out_specs=pl.BlockSpec((n_exp, cap, d + GATE_LANES), lambda e, j, r0: (e, 0, 0),
                               pipeline_mode=pl.Buffered(1)),
    )
    return pl.pallas_call(
        functools.partial(_dispatch_kernel, cap=cap, n_blk=n_blk),
        grid_spec=grid_spec,
        out_shape=jax.ShapeDtypeStruct((N_EXPERTS, cap, d + GATE_LANES), BF16),
        compiler_params=_cparams("parallel", "arbitrary"),
        name="moe_dispatch",
    )(r0, xb, pairs(slot_t), pairs(aff_t))


def _ffn_kernel(xs_ref, w1_ref, w3_ref, w2_ref, o_ref):
    xs = xs_ref[:, :D_MODEL]
    gate = jnp.sum(xs_ref[:, D_MODEL:].astype(F32), axis=1, keepdims=True)
    acc = jnp.zeros((xs.shape[0], D_MODEL), F32)
    for c in range(EXPERT_FF // FF_CHUNK):
        cols = slice(c * FF_CHUNK, (c + 1) * FF_CHUNK)
        h1 = jnp.dot(xs, w1_ref[:, cols].astype(BF16), preferred_element_type=F32)
        h3 = jnp.dot(xs, w3_ref[:, cols].astype(BF16), preferred_element_type=F32)
        h = (jax.nn.silu(h1) * h3).astype(BF16)
        acc = acc + jnp.dot(h, w2_ref[cols, :].astype(BF16), preferred_element_type=F32)
    o_ref[...] = (acc * gate).astype(o_ref.dtype)


def _expert_ffn(xs, w1, w3, w2, layer):
    e, cap, width = xs.shape
    d = D_MODEL
    tile = min(FFN_TILE, cap)
    return pl.pallas_call(
        _ffn_kernel,
        grid=(e, cap // tile),
        in_specs=[pl.BlockSpec((None, tile, width), lambda i, c: (i, c, 0)),
                  pl.BlockSpec((None, None, d, EXPERT_FF), lambda i, c: (layer, i, 0, 0)),
                  pl.BlockSpec((None, None, d, EXPERT_FF), lambda i, c: (layer, i, 0, 0)),
                  pl.BlockSpec((None, None, EXPERT_FF, d), lambda i, c: (layer, i, 0, 0))],
        out_specs=pl.BlockSpec((None, tile, d), lambda i, c: (i, c, 0)),
        out_shape=jax.ShapeDtypeStruct((e, cap, d), BF16),
        compiler_params=_cparams("parallel", "parallel"),
        name="moe_ffn",
    )(xs, w1, w3, w2)


def _combine_kernel(r0_ref, x_ref, slot_ref, lg_ref, lb_ref, ye_ref, o_ref,
                    win_ref, big_ref, acc_ref, sem, big_sem, *, cap):
    j = pl.program_id(0)
    n_blk = pl.num_programs(0)
    buf = j % 2
    experts = range(N_EXPERTS)

    def small_copy(start, b, e):
        return pltpu.make_async_copy(ye_ref.at[e, pl.ds(start, ROUTE_SMALL_WIN)],
                                     win_ref.at[b, pl.ds(e * ROUTE_SMALL_WIN, ROUTE_SMALL_WIN)],
                                     sem.at[b, e])

    starts, fits = _fits_small(r0_ref, experts, j, n_blk, cap, ROUTE_SMALL_WIN)
    nxt = jnp.minimum(j + 1, n_blk - 1)
    nxt_starts, nxt_fits = _fits_small(r0_ref, experts, nxt, n_blk, cap, ROUTE_SMALL_WIN)

    @pl.when((j == 0) & fits)
    def _():
        for e in experts:
            small_copy(starts[e], 0, e).start()

    @pl.when((j + 1 < n_blk) & nxt_fits)
    def _():
        for e in experts:
            small_copy(nxt_starts[e], 1 - buf, e).start()

    slot = slot_ref[...]

    def pick(e, start, width):
        lane = lax.broadcasted_iota(I32, (ROUTE_TILE, width), 1)
        return _one_hot(lane == slot[:, e:e + 1] - start)

    @pl.when(fits)
    def _():
        for e in experts:
            small_copy(starts[e], buf, e).wait()
        picks = jnp.concatenate([pick(e, starts[e], ROUTE_SMALL_WIN) for e in experts], axis=1)
        acc_ref[...] = jnp.dot(picks, win_ref[buf], preferred_element_type=F32)

    @pl.when(jnp.logical_not(fits))
    def _():
        acc = jnp.zeros((ROUTE_TILE, D_MODEL), F32)
        for e in experts:
            start = _window_start(r0_ref[e, j], cap, ROUTE_WIN)
            copy = pltpu.make_async_copy(ye_ref.at[e, pl.ds(start, ROUTE_WIN)], big_ref, big_sem)
            copy.start()
            copy.wait()
            acc = acc + jnp.dot(pick(e, start, ROUTE_WIN), big_ref[...], preferred_element_type=F32)
        acc_ref[...] = acc

    o_ref[...] = _layer_norm(DN_ALPHA * x_ref[...] + acc_ref[...], lg_ref[...], lb_ref[...])


def _combine(x, slot, ye, r0, ln_g, ln_b, cap):
    n, d = x.shape
    n_blk = n // ROUTE_TILE
    grid_spec = pltpu.PrefetchScalarGridSpec(
        num_scalar_prefetch=1,
        grid=(n_blk,),
        in_specs=[pl.BlockSpec((ROUTE_TILE, d), lambda j, r0: (j, 0)),
                  pl.BlockSpec((ROUTE_TILE, N_EXPERTS), lambda j, r0: (j, 0)),
                  pl.BlockSpec((1, d), lambda j, r0: (0, 0)),
                  pl.BlockSpec((1, d), lambda j, r0: (0, 0)),
                  pl.BlockSpec(memory_space=pl.ANY)],
        out_specs=pl.BlockSpec((ROUTE_TILE, d), lambda j, r0: (j, 0)),
        scratch_shapes=[pltpu.VMEM((2, N_EXPERTS * ROUTE_SMALL_WIN, d), BF16),
                        pltpu.VMEM((ROUTE_WIN, d), BF16),
                        pltpu.VMEM((ROUTE_TILE, d), F32),
                        pltpu.SemaphoreType.DMA((2, N_EXPERTS)),
                        pltpu.SemaphoreType.DMA(())],
    )
    return pl.pallas_call(
        functools.partial(_combine_kernel, cap=cap),
        grid_spec=grid_spec,
        out_shape=jax.ShapeDtypeStruct((n, d), F32),
        compiler_params=_cparams("arbitrary"),
        name="moe_combine",
    )(r0, x, slot, ln_g.reshape(1, d), ln_b.reshape(1, d), ye)


def _moe_layer(x, xb, aff_t, w1, w3, w2, layer, ln_g, ln_b):
    bsz, seq, d = x.shape
    n = bsz * seq
    cap = EC_CAPACITY_FACTOR * n // N_EXPERTS
    slot_t, r0 = _route(aff_t, cap)
    xs = _dispatch(xb.reshape(n, d), slot_t, aff_t, r0, cap)
    ye = _expert_ffn(xs, w1, w3, w2, layer)
    out = _combine(x.reshape(n, d), slot_t.T, ye, r0, ln_g, ln_b, cap)
    return out.reshape(bsz, seq, d)


def _trunk(x, mem, p):
    bsz = x.shape[0]
    m = mem.shape[1]
    for i in range(DEPTH):
        if i % 2 == 0:
            x = _s5_layer(x, p["a_w_in"][i // 2], p["s5_ops"][i // 2], p["a_w_glu"][i // 2],
                          p["ln_g"][i, 0], p["ln_b"][i, 0])
        else:
            x = _na_layer(x, p["b_w_qkv"][i // 2], p["na_bias"][i // 2], p["b_w_o"][i // 2],
                          p["ln_g"][i, 0], p["ln_b"][i, 0])
        kv = _matmul(mem.reshape(bsz * m, D_MODEL), p["m_w_kv"][i], BF16).reshape(bsz, m, 2 * D_MODEL)
        x, xb, aff_t = _xattn_router(x, kv[:, :, :D_MODEL], kv[:, :, D_MODEL:], p["m_w_q"][i], p["m_w_o"][i],
                                     p["ln_g"][i, 1], p["ln_b"][i, 1], p["e_w_router_t"][i])
        x = _moe_layer(x, xb, aff_t, p["e_w1"], p["e_w3"], p["e_w2"], i,
                       p["ln_g"][i, 2], p["ln_b"][i, 2])
    return x


def kernel(x_prompt, x_sample, mem_prompt, mem_sample, a_w_in, a_lam_re, a_lam_im, a_log_dt, a_b_re, a_b_im, a_c_re, a_c_im, a_d, a_w_glu, b_w_qkv, b_rpb, b_w_o, m_w_q, m_w_kv, m_w_o, e_w_router, e_w1, e_w3, e_w2, ln_g, ln_b):
    bf = lambda w: w.astype(BF16)
    p = {
        "a_w_in": bf(a_w_in), "a_w_glu": bf(a_w_glu),
        "s5_ops": [_s5_operators(a_lam_re[j], a_lam_im[j], a_log_dt[j], a_b_re[j], a_b_im[j],
                                 a_c_re[j], a_c_im[j], a_d[j]) for j in range(a_w_in.shape[0])],
        "b_w_qkv": bf(b_w_qkv), "b_w_o": bf(b_w_o),
        "na_bias": [_na_bias_table(b_rpb[j]) for j in range(b_rpb.shape[0])],
        "m_w_q": bf(m_w_q), "m_w_kv": bf(m_w_kv), "m_w_o": bf(m_w_o),
        "e_w_router_t": bf(jnp.swapaxes(e_w_router, 1, 2)),
        "e_w1": e_w1, "e_w3": e_w3, "e_w2": e_w2,
        "ln_g": ln_g.astype(F32), "ln_b": ln_b.astype(F32),
    }
    return (_trunk(x_prompt, mem_prompt, p), _trunk(x_sample, mem_sample, p))
```

```python
import functools
import math

import jax
import jax.numpy as jnp
from jax import lax
from jax.experimental import pallas as pl
from jax.experimental.pallas import tpu as pltpu

F32 = jnp.float32
BF16 = jnp.bfloat16
I32 = jnp.int32

D_MODEL = 1024
DEPTH = 2
GRID_W = 64
S5_GROUP = 16
S5_GROUPS = D_MODEL // S5_GROUP
S5_STATE = 64
S5_CHUNK = 16
S5_GROUPS_PER_STEP = 4
NA_HEADS = 16
NA_HEAD_DIM = D_MODEL // NA_HEADS
NA_WIN_ROWS = 8
NA_WIN_COLS = 16
NA_QCOLS = 16
NA_KCOLS = 32
NA_ROW_BLOCK = 8
NA_KEY_ROWS = 16
NA_MASKED = -1e30
MEM_HEADS = 4
MEM_HEAD_DIM = D_MODEL // MEM_HEADS
N_EXPERTS = 16
EXPERT_FF = 2048
EC_CAPACITY_FACTOR = 2
ROUTE_TILE = 256
ROUTE_ALIGN = 16
ROUTE_WIN = ROUTE_TILE + ROUTE_ALIGN
ROUTE_SMALL_WIN = 80
DISPATCH_SMALL_WIN = 64
DISPATCH_VMEM_BYTES = 40 * 1024 * 1024
DISPATCH_TOKENS = 2048
GATE_LANES = 128
FFN_TILE = 512
FF_CHUNK = 512
DN_ALPHA = (2.0 * DEPTH) ** 0.25
LN_EPS = 1e-5

ROW_TILE = 512
XATTN_TILE = 1024
VMEM_LIMIT = 56 * 1024 * 1024


def _cparams(*sem):
    return pltpu.CompilerParams(dimension_semantics=sem, vmem_limit_bytes=VMEM_LIMIT)


def _layer_norm(v, g, b):
    mu = jnp.mean(v, axis=-1, keepdims=True)
    c = v - mu
    var = jnp.mean(c * c, axis=-1, keepdims=True)
    return c * lax.rsqrt(var + LN_EPS) * g + b


def _matmul_kernel(x_ref, w_ref, o_ref):
    o_ref[...] = jnp.dot(x_ref[...].astype(BF16), w_ref[...],
                         preferred_element_type=F32).astype(o_ref.dtype)


def _matmul(x, w, out_dtype, tile=ROW_TILE):
    n, k = x.shape
    m = w.shape[1]
    tile = min(tile, n)
    return pl.pallas_call(
        _matmul_kernel,
        grid=(n // tile,),
        in_specs=[pl.BlockSpec((tile, k), lambda i: (i, 0)),
                  pl.BlockSpec((k, m), lambda i: (0, 0))],
        out_specs=pl.BlockSpec((tile, m), lambda i: (i, 0)),
        out_shape=jax.ShapeDtypeStruct((n, m), out_dtype),
        compiler_params=_cparams("parallel"),
        name="matmul",
    )(x, w)


def _s5_ops_kernel(pw_re_ref, pw_im_ref, pws_re_ref, pws_im_ref, bt_re_ref, bt_im_ref,
                   btb_re_ref, btb_im_ref, ct_re_ref, ct_im_ref, t_ref, b_ref, c_ref):
    n = S5_CHUNK
    w = n * S5_GROUP
    exact = lax.Precision.HIGHEST
    lane = lax.broadcasted_iota(I32, (S5_STATE, w), 1)
    k_blk = lane // S5_GROUP
    c_idx = lane % S5_GROUP
    t_lane = lax.broadcasted_iota(I32, (S5_GROUP, w), 1)

    def spread_k(pw, k_of_lane):
        out = jnp.zeros((S5_STATE, w), F32)
        for k in range(n + 1):
            out = jnp.where(k_of_lane == k, pw[:, k:k + 1], out)
        return out

    def spread_c(ct):
        out = jnp.zeros((S5_STATE, w), F32)
        for c in range(S5_GROUP):
            out = jnp.where(c_idx == c, ct[:, c:c + 1], out)
        return out

    for gi in range(t_ref.shape[0]):
        kern = []
        for d in range(2):
            pw_re, pw_im = pw_re_ref[d, gi], pw_im_ref[d, gi]
            c_re, c_im = spread_c(ct_re_ref[d, gi]), spread_c(ct_im_ref[d, gi])
            lag = k_blk if d == 0 else (n - 1) - k_blk
            out = k_blk + 1 if d == 0 else n - k_blk
            a_re, a_im = spread_k(pw_re, lag), spread_k(pw_im, lag)
            cp_re, cp_im = c_re * a_re - c_im * a_im, c_re * a_im + c_im * a_re
            kern.append(jnp.dot(bt_re_ref[d, gi], cp_re, precision=exact, preferred_element_type=F32)
                        - jnp.dot(bt_im_ref[d, gi], cp_im, precision=exact, preferred_element_type=F32))
            a_re, a_im = spread_k(pw_re, out), spread_k(pw_im, out)
            c_ref[gi, d * S5_STATE:(d + 1) * S5_STATE, :] = (c_re * a_re - c_im * a_im).astype(BF16)
            c_ref[gi, (2 + d) * S5_STATE:(3 + d) * S5_STATE, :] = (-(c_re * a_im + c_im * a_re)).astype(BF16)
        kern_f, kern_b = kern
        bt_re, bt_im = btb_re_ref[gi], btb_im_ref[gi]
        for s in range(n):
            rows = slice(s * S5_GROUP, (s + 1) * S5_GROUP)
            fwd = pltpu.roll(kern_f, s * S5_GROUP, 1) if s else kern_f
            fwd = jnp.where(t_lane >= s * S5_GROUP, fwd, 0.0)
            shift = ((s + 1) * S5_GROUP) % w
            bwd = pltpu.roll(kern_b, shift, 1) if shift else kern_b
            bwd = jnp.where(t_lane < (s + 1) * S5_GROUP, bwd, 0.0)
            t_ref[gi, rows, :] = (fwd + bwd).astype(BF16)
            p_re, p_im = pws_re_ref[gi, s:s + 1, :], pws_im_ref[gi, s:s + 1, :]
            b_ref[gi, rows, :2 * S5_STATE] = (p_re * bt_re - p_im * bt_im).astype(BF16)
            b_ref[gi, rows, 2 * S5_STATE:] = (p_re * bt_im + p_im * bt_re).astype(BF16)


def _s5_operators(lam_re, lam_im, log_dt, b_re, b_im, c_re, c_im, d_skip):
    n = S5_CHUNK
    dt = jnp.exp(log_dt.astype(F32))[..., None]
    lr = lam_re.astype(F32) * dt
    li = lam_im.astype(F32) * dt
    k = jnp.arange(n + 1, dtype=F32)
    mag = jnp.exp(lr[..., None] * k)
    pw_re = mag * jnp.cos(li[..., None] * k)
    pw_im = mag * jnp.sin(li[..., None] * k)
    x = pw_re[..., 1] - 1.0
    y = pw_im[..., 1]
    a = lam_re.astype(F32)
    b = lam_im.astype(F32)
    den = a * a + b * b
    q_re = ((x * a + y * b) / den)[..., None]
    q_im = ((y * a - x * b) / den)[..., None]
    bb_re = q_re * b_re.astype(F32) - q_im * b_im.astype(F32)
    bb_im = q_re * b_im.astype(F32) + q_im * b_re.astype(F32)
    bt_re = jnp.swapaxes(bb_re, 2, 3)
    bt_im = jnp.swapaxes(bb_im, 2, 3)
    ct_re = jnp.swapaxes(c_re.astype(F32), 2, 3)
    ct_im = jnp.swapaxes(c_im.astype(F32), 2, 3)
    both = lambda fwd, bwd: jnp.concatenate([fwd, bwd], axis=-1)
    pwt_re = jnp.swapaxes(pw_re[..., :n], 2, 3)
    pwt_im = jnp.swapaxes(pw_im[..., :n], 2, 3)
    pws_re = both(jnp.flip(pwt_re[0], axis=1), pwt_re[1])
    pws_im = both(jnp.flip(pwt_im[0], axis=1), pwt_im[1])
    gb = S5_GROUPS_PER_STEP
    per_dir = lambda shape: pl.BlockSpec((2, gb) + shape, lambda i: (0, i, 0, 0))
    per_grp = lambda shape: pl.BlockSpec((gb,) + shape, lambda i: (i, 0, 0))
    w = n * S5_GROUP
    t_op, b_op, c_op = pl.pallas_call(
        _s5_ops_kernel,
        grid=(S5_GROUPS // gb,),
        in_specs=[per_dir((S5_STATE, n + 1)), per_dir((S5_STATE, n + 1)),
                  per_grp((n, 2 * S5_STATE)), per_grp((n, 2 * S5_STATE)),
                  per_dir((S5_GROUP, S5_STATE)), per_dir((S5_GROUP, S5_STATE)),
                  per_grp((S5_GROUP, 2 * S5_STATE)), per_grp((S5_GROUP, 2 * S5_STATE)),
                  per_dir((S5_STATE, S5_GROUP)), per_dir((S5_STATE, S5_GROUP))],
        out_specs=[per_grp((w, w))] * 3,
        out_shape=[jax.ShapeDtypeStruct((S5_GROUPS, w, w), BF16)] * 3,
        compiler_params=_cparams("parallel"),
        name="s5_operators",
    )(pw_re, pw_im, pws_re, pws_im, bt_re, bt_im, both(bt_re[0], bt_re[1]), both(bt_im[0], bt_im[1]),
      ct_re, ct_im)
    a_re = jnp.concatenate([pw_re[0][..., n], pw_re[1][..., n]], axis=-1)
    a_im = jnp.concatenate([pw_im[0][..., n], pw_im[1][..., n]], axis=-1)
    a_op = jnp.stack([a_re, a_im], axis=1)
    a_op = jnp.broadcast_to(a_op[:, :, None, :], (S5_GROUPS, 2, 8, 128))
    d_op = jnp.tile(d_skip.astype(F32).reshape(S5_GROUPS, 1, 16), (1, 1, n))
    return t_op, b_op, c_op, a_op, d_op


def _s5_kernel(u_ref, t_ref, b_ref, c_ref, a_ref, d_ref, o_ref, s_scr, hc_scr, *, n_tiles):
    gb = u_ref.shape[0]
    for gi in range(gb):
        s_scr[gi] = jnp.dot(u_ref[gi].astype(BF16), b_ref[gi], preferred_element_type=F32)

    row = lax.broadcasted_iota(I32, (8, 128), 0)
    lane = lax.broadcasted_iota(I32, (8, 128), 1)
    low_rows = row < 4
    fwd_lanes = lane < S5_STATE
    a_re = [a_ref[gi, 0] for gi in range(gb)]
    a_im = [a_ref[gi, 1] for gi in range(gb)]

    def cmul_add(gi, h_re, h_im, v_re, v_im):
        return (a_re[gi] * h_re - a_im[gi] * h_im + v_re,
                a_re[gi] * h_im + a_im[gi] * h_re + v_im)

    def half_steps(gi, k, c_re, c_im):
        rows = pl.ds(pl.multiple_of(k * 8, 8), 8)
        v_re = s_scr[gi, rows, 0:128]
        v_im = s_scr[gi, rows, 128:256]
        h1_re, h1_im = cmul_add(gi, c_re, c_im, v_re, v_im)
        h1s_re = pltpu.roll(h1_re, 4, 0)
        h1s_im = pltpu.roll(h1_im, 4, 0)
        h2_re, h2_im = cmul_add(gi, h1s_re, h1s_im, v_re, v_im)
        return rows, h1s_re, h1s_im, pltpu.roll(h2_re, 4, 0), pltpu.roll(h2_im, 4, 0)

    def fwd_body(k, carry):
        out = []
        for gi in range(gb):
            c_re, c_im = carry[2 * gi], carry[2 * gi + 1]
            rows, h1s_re, h1s_im, n_re, n_im = half_steps(gi, k, c_re, c_im)
            hc_scr[gi, rows, 0:128] = jnp.where(low_rows, c_re, h1s_re)
            hc_scr[gi, rows, 128:256] = jnp.where(low_rows, c_im, h1s_im)
            out += [n_re, n_im]
        return tuple(out)

    def bwd_body(i, carry):
        k = n_tiles - 1 - i
        out = []
        for gi in range(gb):
            c_re, c_im = carry[2 * gi], carry[2 * gi + 1]
            rows, h1s_re, h1s_im, n_re, n_im = half_steps(gi, k, c_re, c_im)
            new_re = jnp.where(low_rows, h1s_re, c_re)
            new_im = jnp.where(low_rows, h1s_im, c_im)
            hc_scr[gi, rows, 0:128] = jnp.where(fwd_lanes, hc_scr[gi, rows, 0:128], new_re)
            hc_scr[gi, rows, 128:256] = jnp.where(fwd_lanes, hc_scr[gi, rows, 128:256], new_im)
            out += [n_re, n_im]
        return tuple(out)

    zeros = tuple(jnp.zeros((8, 128), F32) for _ in range(2 * gb))
    lax.fori_loop(0, n_tiles, fwd_body, zeros)
    lax.fori_loop(0, n_tiles, bwd_body, zeros)

    for gi in range(gb):
        u = u_ref[gi]
        y = (jnp.dot(u.astype(BF16), t_ref[gi], preferred_element_type=F32)
             + jnp.dot(hc_scr[gi].astype(BF16), c_ref[gi], preferred_element_type=F32)
             + d_ref[gi] * u)
        o_ref[gi] = jax.nn.gelu(y).astype(o_ref.dtype)


def _s5_scan(u_rows, ops):
    t_op, b_op, c_op, a_op, d_op = ops
    g, m, w = u_rows.shape
    gb = S5_GROUPS_PER_STEP
    spec3 = lambda shape: pl.BlockSpec((gb,) + shape, lambda i: (i,) + (0,) * len(shape))
    return pl.pallas_call(
        functools.partial(_s5_kernel, n_tiles=m // 8),
        grid=(g // gb,),
        in_specs=[spec3((m, w)), spec3((w, w)), spec3((w, w)), spec3((w, w)),
                  spec3((2, 8, 128)), spec3((1, w))],
        out_specs=spec3((m, w)),
        out_shape=jax.ShapeDtypeStruct((g, m, w), F32),
        scratch_shapes=[pltpu.VMEM((gb, m, w), F32), pltpu.VMEM((gb, m, w), F32)],
        compiler_params=_cparams("parallel"),
        name="s5_scan",
    )(u_rows, t_op, b_op, c_op, a_op, d_op)


def _proj_ln_kernel(h_ref, x_ref, w_ref, lg_ref, lb_ref, o_ref):
    h = jnp.dot(h_ref[...], w_ref[...], preferred_element_type=F32)
    o_ref[...] = _layer_norm(DN_ALPHA * x_ref[...] + h, lg_ref[...], lb_ref[...])


def _residual_ln(body, h, x, w, ln_g, ln_b, name):
    n, d = x.shape
    tile = min(ROW_TILE, n)
    row = lambda width: pl.BlockSpec((tile, width), lambda i: (i, 0))
    const = lambda shape: pl.BlockSpec(shape, lambda i: (0, 0))
    return pl.pallas_call(
        body,
        grid=(n // tile,),
        in_specs=[row(h.shape[1]), row(d), const(w.shape), const((1, d)), const((1, d))],
        out_specs=row(d),
        out_shape=jax.ShapeDtypeStruct((n, d), F32),
        compiler_params=_cparams("parallel"),
        name=name,
    )(h, x, w, ln_g.reshape(1, d), ln_b.reshape(1, d))


def _block_transpose(x):
    rows, lanes = x.ndim - 2, x.ndim - 1
    row = lax.broadcasted_iota(I32, x.shape, rows)
    blk = lax.broadcasted_iota(I32, x.shape, lanes) // S5_GROUP
    for d in (1, 2, 4):
        up = pltpu.roll(pltpu.roll(x, 8 - d, rows), S5_GROUP * d, lanes)
        down = pltpu.roll(pltpu.roll(x, d, rows), 128 - S5_GROUP * d, lanes)
        col_bit = (blk & d) != 0
        x = jnp.where((row & d) == (blk & d), x, jnp.where(col_bit, up, down))
    return x


def _s5_in_kernel(x_ref, w_ref, o_ref):
    bsz, steps, d = x_ref.shape
    n_chunk = steps // S5_CHUNK
    u = jnp.dot(x_ref[...].reshape(bsz * steps, d).astype(BF16), w_ref[...], preferred_element_type=F32)
    u = u.reshape(bsz, n_chunk, S5_CHUNK, d)
    for s in range(S5_CHUNK // 8):
        for c in range(d // 128):
            w = _block_transpose(u[:, :, s * 8:(s + 1) * 8, c * 128:(c + 1) * 128])
            for jj in range(n_chunk):
                for b in range(bsz):
                    o_ref[c * 8:(c + 1) * 8, jj * bsz + b, s * 128:(s + 1) * 128] = w[b, jj]


def _s5_glu_ln_kernel(g_ref, x_ref, w_ref, lg_ref, lb_ref, o_ref, tok_scr):
    bsz, steps, d = x_ref.shape
    n_chunk = steps // S5_CHUNK
    for s in range(S5_CHUNK // 8):
        for c in range(d // 128):
            w = jnp.stack([jnp.stack([g_ref[c * 8:(c + 1) * 8, jj * bsz + b, s * 128:(s + 1) * 128]
                                      for jj in range(n_chunk)]) for b in range(bsz)])
            tok_scr[:, :, s * 8:(s + 1) * 8, c * 128:(c + 1) * 128] = _block_transpose(w)
    g = tok_scr[...].reshape(bsz * steps, d).astype(BF16)
    z = jnp.dot(g, w_ref[...], preferred_element_type=F32)
    h = z[:, :d] * jax.nn.sigmoid(z[:, d:])
    x = x_ref[...].reshape(bsz * steps, d)
    o_ref[...] = _layer_norm(DN_ALPHA * x + h, lg_ref[...], lb_ref[...]).reshape(bsz, steps, d)


def _s5_layer(x, w_in, ops, w_glu, ln_g, ln_b):
    bsz, seq, d = x.shape
    j = seq // S5_CHUNK
    steps = ROW_TILE // bsz
    n_chunk = steps // S5_CHUNK
    width = S5_CHUNK * S5_GROUP
    tok_spec = pl.BlockSpec((bsz, steps, d), lambda i: (0, i, 0))
    row_spec = pl.BlockSpec((S5_GROUPS, n_chunk * bsz, width), lambda i: (0, i, 0))
    const = lambda shape: pl.BlockSpec(shape, lambda i: (0, 0))
    u_rows = pl.pallas_call(
        _s5_in_kernel,
        grid=(seq // steps,),
        in_specs=[tok_spec, const((d, d))],
        out_specs=row_spec,
        out_shape=jax.ShapeDtypeStruct((S5_GROUPS, j * bsz, width), F32),
        compiler_params=_cparams("parallel"),
        name="s5_in_proj",
    )(x, w_in)
    g_rows = _s5_scan(u_rows, ops)
    return pl.pallas_call(
        _s5_glu_ln_kernel,
        grid=(seq // steps,),
        in_specs=[row_spec, tok_spec, const((d, 2 * d)), const((1, d)), const((1, d))],
        out_specs=tok_spec,
        out_shape=jax.ShapeDtypeStruct((bsz, seq, d), F32),
        scratch_shapes=[pltpu.VMEM((bsz, n_chunk, S5_CHUNK, d), F32)],
        compiler_params=_cparams("parallel"),
        name="s5_glu_ln",
    )(g_rows, x, w_glu, ln_g.reshape(1, d), ln_b.reshape(1, d))


def _na_key_col_starts():
    n_cb = GRID_W // NA_QCOLS
    return [min(max(n * NA_QCOLS - NA_WIN_COLS // 2, 0), GRID_W - NA_KCOLS) for n in range(n_cb)]


def _na_bias_table(rpb):
    exact = lax.Precision.HIGHEST
    n_cb = GRID_W // NA_QCOLS
    starts = jnp.asarray(_na_key_col_starts(), I32)[:, None, None]
    half = NA_WIN_ROWS // 2
    rl = jnp.arange(NA_ROW_BLOCK)[:, None]
    kl = jnp.arange(NA_KEY_ROWS)[None, :]
    dr = jnp.clip(kl - rl + (NA_WIN_ROWS - 1 - half), 0, 2 * NA_WIN_ROWS - 2)
    qc = jnp.arange(n_cb)[:, None, None] * NA_QCOLS + jnp.arange(NA_QCOLS)[None, :, None]
    kc = starts + jnp.arange(NA_KCOLS)[None, None, :]
    dc = jnp.clip(kc - qc + NA_WIN_COLS - 1, 0, 2 * NA_WIN_COLS - 2)
    pick_dc = (dc[..., None] == jnp.arange(2 * NA_WIN_COLS - 1)).astype(F32)
    pick_dr = (dr[..., None] == jnp.arange(2 * NA_WIN_ROWS - 1)).astype(F32)
    by_col = jnp.einsum('hrc,nqkc->hrnqk', rpb.astype(F32), pick_dc, precision=exact)
    bias = jnp.einsum('hrnqk,alr->nhaqlk', by_col, pick_dr, precision=exact)
    nq = NA_ROW_BLOCK * NA_QCOLS
    nk = NA_KEY_ROWS * NA_KCOLS
    bias = bias.reshape(n_cb, 1, NA_HEADS // 2, 2 * nq, nk)
    row = jnp.arange(2 * nq)[:, None]
    col = jnp.arange(nk)[None, :]
    f_rl, f_qcl = (row % nq) // NA_QCOLS, row % NA_QCOLS
    f_kl, f_kcl = col // NA_KCOLS, col % NA_KCOLS
    f_rs = jnp.stack([jnp.maximum(f_rl - half, 0), f_rl - half, jnp.minimum(f_rl - half, 0)])
    f_row_ok = (f_kl - half >= f_rs) & (f_kl - half < f_rs + NA_WIN_ROWS)
    blk = jnp.arange(n_cb)[:, None, None]
    f_qc = blk * NA_QCOLS + f_qcl
    f_kc = starts + f_kcl
    f_ws = jnp.clip(f_qc - NA_WIN_COLS // 2, 0, GRID_W - NA_WIN_COLS)
    f_col_ok = (f_kc >= f_ws) & (f_kc < f_ws + NA_WIN_COLS)
    ok = f_row_ok[None, :, None] & f_col_ok[:, None, None]
    return jnp.where(ok, bias, NA_MASKED)


def _qkv_kernel(x_ref, w_ref, q_ref, k_ref, v_ref):
    qkv = jnp.dot(x_ref[...].astype(BF16), w_ref[...], preferred_element_type=F32)
    q_ref[...] = (qkv[:, :D_MODEL] * (NA_HEAD_DIM ** -0.5)).astype(BF16)
    k = qkv[:, D_MODEL:2 * D_MODEL]
    v = qkv[:, 2 * D_MODEL:]
    for r in range(NA_ROW_BLOCK):
        for n, start in enumerate(_na_key_col_starts()):
            lo = r * GRID_W + start
            k_ref[r, n] = k[lo:lo + NA_KCOLS].astype(BF16)
            v_ref[r, n] = v[lo:lo + NA_KCOLS].astype(BF16)


def _na_kernel(q_ref, kp_ref, kc_ref, kn_ref, vp_ref, vc_ref, vn_ref, bias_ref, o_ref, k_scr, v_scr):
    nq = NA_ROW_BLOCK * NA_QCOLS
    quarter = 4 * NA_KCOLS
    k_scr[0:quarter] = kp_ref[...].reshape(quarter, D_MODEL)
    k_scr[quarter:3 * quarter] = kc_ref[...].reshape(2 * quarter, D_MODEL)
    k_scr[3 * quarter:] = kn_ref[...].reshape(quarter, D_MODEL)
    v_scr[0:quarter] = vp_ref[...].reshape(quarter, D_MODEL)
    v_scr[quarter:3 * quarter] = vc_ref[...].reshape(2 * quarter, D_MODEL)
    v_scr[3 * quarter:] = vn_ref[...].reshape(quarter, D_MODEL)
    q = q_ref[...].reshape(nq, D_MODEL)
    first_head = lax.broadcasted_iota(I32, (nq, 128), 1) < NA_HEAD_DIM
    zero = jnp.zeros((nq, 128), BF16)
    for hp in range(NA_HEADS // 2):
        lanes = slice(hp * 128, (hp + 1) * 128)
        q2 = q[:, lanes]
        qs = jnp.concatenate([jnp.where(first_head, q2, zero), jnp.where(first_head, zero, q2)], axis=0)
        s = lax.dot_general(qs, k_scr[:, lanes], (((1,), (1,)), ((), ())),
                            preferred_element_type=F32) + bias_ref[hp]
        m = jnp.max(s, axis=-1, keepdims=True)
        p = jnp.exp(s - m)
        l = jnp.sum(p, axis=-1, keepdims=True)
        o = jnp.dot(p.astype(BF16), v_scr[:, lanes], preferred_element_type=F32) / l
        o_ref[:, :, lanes] = jnp.where(first_head, o[:nq], o[nq:]).astype(BF16).reshape(
            NA_ROW_BLOCK, NA_QCOLS, 128)


def _na_layer(x, w_qkv, bias_table, w_o, ln_g, ln_b):
    bsz, seq, d = x.shape
    rows = seq // GRID_W
    n_cb = GRID_W // NA_QCOLS
    n_rb = rows // NA_ROW_BLOCK
    assert n_rb >= 2
    tile = NA_ROW_BLOCK * GRID_W
    q, kx, vx = pl.pallas_call(
        _qkv_kernel,
        grid=(bsz, n_rb),
        in_specs=[pl.BlockSpec((None, tile, d), lambda b, i: (b, i, 0)),
                  pl.BlockSpec((d, 3 * d), lambda b, i: (0, 0))],
        out_specs=[pl.BlockSpec((None, tile, d), lambda b, i: (b, i, 0)),
                   pl.BlockSpec((None, NA_ROW_BLOCK, n_cb, NA_KCOLS, d), lambda b, i: (b, i, 0, 0, 0)),
                   pl.BlockSpec((None, NA_ROW_BLOCK, n_cb, NA_KCOLS, d), lambda b, i: (b, i, 0, 0, 0))],
        out_shape=[jax.ShapeDtypeStruct((bsz, seq, d), BF16),
                   jax.ShapeDtypeStruct((bsz, rows, n_cb, NA_KCOLS, d), BF16),
                   jax.ShapeDtypeStruct((bsz, rows, n_cb, NA_KCOLS, d), BF16)],
        compiler_params=_cparams("parallel", "parallel"),
        name="na_qkv",
    )(x, w_qkv)

    q5 = q.reshape(bsz, rows, n_cb, NA_QCOLS, d)
    n_half = rows // 4
    half_view = lambda a: a.reshape(bsz, n_half, 4, n_cb, NA_KCOLS, d)
    full_view = lambda a: a.reshape(bsz, n_rb, NA_ROW_BLOCK, n_cb, NA_KCOLS, d)
    prev_spec = pl.BlockSpec((None, None, 4, None, NA_KCOLS, d),
                             lambda n, b, i: (b, jnp.maximum(2 * i - 1, 0), 0, n, 0, 0))
    cur_spec = pl.BlockSpec((None, None, NA_ROW_BLOCK, None, NA_KCOLS, d),
                            lambda n, b, i: (b, i, 0, n, 0, 0))
    next_spec = pl.BlockSpec((None, None, 4, None, NA_KCOLS, d),
                             lambda n, b, i: (b, jnp.minimum(2 * i + 2, n_half - 1), 0, n, 0, 0))
    kind = lambda i: jnp.where(i == 0, 0, jnp.where(i == n_rb - 1, 2, 1))
    nk = NA_KEY_ROWS * NA_KCOLS
    o5 = pl.pallas_call(
        _na_kernel,
        grid=(n_cb, bsz, n_rb),
        in_specs=[pl.BlockSpec((None, NA_ROW_BLOCK, None, NA_QCOLS, d), lambda n, b, i: (b, i, n, 0, 0)),
                  prev_spec, cur_spec, next_spec, prev_spec, cur_spec, next_spec,
                  pl.BlockSpec((None, None) + bias_table.shape[2:],
                               lambda n, b, i: (n, kind(i), 0, 0, 0))],
        out_specs=pl.BlockSpec((None, NA_ROW_BLOCK, None, NA_QCOLS, d), lambda n, b, i: (b, i, n, 0, 0)),
        out_shape=jax.ShapeDtypeStruct((bsz, rows, n_cb, NA_QCOLS, d), BF16),
        scratch_shapes=[pltpu.VMEM((nk, d), BF16), pltpu.VMEM((nk, d), BF16)],
        compiler_params=_cparams("parallel", "parallel", "parallel"),
        name="na_attn",
    )(q5, half_view(kx), full_view(kx), half_view(kx), half_view(vx), full_view(vx), half_view(vx),
      bias_table)
    o = o5.reshape(bsz * seq, d)
    return _residual_ln(_proj_ln_kernel, o, x.reshape(bsz * seq, d), w_o, ln_g, ln_b,
                        "na_out_ln").reshape(bsz, seq, d)


def _xattn_kernel(x_ref, k_ref, v_ref, wq_ref, wo_ref, lg_ref, lb_ref, wr_ref, o_ref, ob_ref, aff_ref):
    x = x_ref[...]
    q = jnp.dot(x.astype(BF16), wq_ref[...], preferred_element_type=F32).astype(BF16)
    heads = []
    for h in range(MEM_HEADS):
        lanes = slice(h * MEM_HEAD_DIM, (h + 1) * MEM_HEAD_DIM)
        s = lax.dot_general(q[:, lanes], k_ref[:, lanes], (((1,), (1,)), ((), ())),
                            preferred_element_type=F32) * (MEM_HEAD_DIM ** -0.5)
        m = jnp.max(s, axis=-1, keepdims=True)
        p = jnp.exp(s - m)
        p = p / jnp.sum(p, axis=-1, keepdims=True)
        heads.append(jnp.dot(p.astype(BF16), v_ref[:, lanes], preferred_element_type=F32).astype(BF16))
    o = jnp.concatenate(heads, axis=-1)
    r = jnp.dot(o, wo_ref[...], preferred_element_type=F32)
    y = _layer_norm(DN_ALPHA * x + r, lg_ref[...], lb_ref[...])
    o_ref[...] = y
    yb = y.astype(BF16)
    ob_ref[...] = yb
    logits = lax.dot_general(wr_ref[...], yb, (((1,), (1,)), ((), ())), preferred_element_type=F32)
    e = jnp.exp(logits - jnp.max(logits, axis=0, keepdims=True))
    aff_ref[...] = e / jnp.sum(e, axis=0, keepdims=True)


def _xattn_router(x, mem_k, mem_v, w_q, w_o, ln_g, ln_b, w_router_t):
    bsz, seq, d = x.shape
    m = mem_k.shape[1]
    tile = min(XATTN_TILE, seq)
    n_t = seq // tile
    const = lambda shape: pl.BlockSpec(shape, lambda b, i: (0, 0))
    return pl.pallas_call(
        _xattn_kernel,
        grid=(bsz, n_t),
        in_specs=[pl.BlockSpec((None, tile, d), lambda b, i: (b, i, 0)),
                  pl.BlockSpec((None, m, d), lambda b, i: (b, 0, 0)),
                  pl.BlockSpec((None, m, d), lambda b, i: (b, 0, 0)),
                  const((d, d)), const((d, d)), const((1, d)), const((1, d)), const((N_EXPERTS, d))],
        out_specs=[pl.BlockSpec((None, tile, d), lambda b, i: (b, i, 0)),
                   pl.BlockSpec((None, tile, d), lambda b, i: (b, i, 0)),
                   pl.BlockSpec((N_EXPERTS, tile), lambda b, i: (0, b * n_t + i))],
        out_shape=[jax.ShapeDtypeStruct((bsz, seq, d), F32),
                   jax.ShapeDtypeStruct((bsz, seq, d), BF16),
                   jax.ShapeDtypeStruct((N_EXPERTS, bsz * seq), F32)],
        compiler_params=_cparams("parallel", "parallel"),
        name="xattn_router",
    )(x, mem_k, mem_v, w_q, w_o, ln_g.reshape(1, d), ln_b.reshape(1, d), w_router_t)


def _route_kernel(aff_ref, slot_ref, r0_ref, *, cap):
    n = aff_ref.shape[1]
    n_blk = n // ROUTE_TILE
    aff = aff_ref[...]

    def count(mask):
        return jnp.sum(jnp.where(mask, 1.0, 0.0), axis=1, keepdims=True)

    def search(i, bits):
        cand = bits | lax.shift_left(jnp.int32(1), 30 - i)
        ge = aff >= lax.bitcast_convert_type(cand, F32)
        return jnp.where(count(ge) >= cap, cand, bits)

    tau = lax.bitcast_convert_type(
        lax.fori_loop(0, 31, search, jnp.zeros((N_EXPERTS, 1), I32)), F32)
    need = cap - count(aff > tau)
    upper = (lax.broadcasted_iota(I32, (ROUTE_TILE, ROUTE_TILE), 0)
             < lax.broadcasted_iota(I32, (ROUTE_TILE, ROUTE_TILE), 1)).astype(BF16)
    blk_lane = lax.broadcasted_iota(I32, (N_EXPERTS, n_blk), 1)

    r0_ref[...] = jnp.zeros_like(r0_ref)

    def block(kb, carry):
        c_eq, c_sel = carry
        cols = pl.ds(pl.multiple_of(kb * ROUTE_TILE, ROUTE_TILE), ROUTE_TILE)
        b = aff_ref[:, cols]
        eq = b == tau
        eq_f = jnp.where(eq, 1.0, 0.0)
        eq_rank = c_eq + jnp.dot(eq_f.astype(BF16), upper, preferred_element_type=F32)
        sel = (b > tau) | (eq & (eq_rank < need))
        sel_f = jnp.where(sel, 1.0, 0.0)
        rank = c_sel + jnp.dot(sel_f.astype(BF16), upper, preferred_element_type=F32)
        slot_ref[:, cols] = jnp.where(sel, rank.astype(I32), -1)
        r0_ref[...] = jnp.where(blk_lane == kb, c_sel.astype(I32), r0_ref[...])
        return (c_eq + jnp.sum(eq_f, axis=1, keepdims=True),
                c_sel + jnp.sum(sel_f, axis=1, keepdims=True))

    zero = jnp.zeros((N_EXPERTS, 1), F32)
    lax.fori_loop(0, n_blk, block, (zero, zero))


def _route(aff_t, cap):
    e, n = aff_t.shape
    n_blk = n // ROUTE_TILE
    return pl.pallas_call(
        functools.partial(_route_kernel, cap=cap),
        out_shape=[jax.ShapeDtypeStruct((e, n), I32), jax.ShapeDtypeStruct((e, n_blk), I32)],
        compiler_params=pltpu.CompilerParams(vmem_limit_bytes=VMEM_LIMIT),
        name="route",
    )(aff_t)


def _window_start(r0, cap, width):
    start = jnp.minimum((r0 // ROUTE_ALIGN) * ROUTE_ALIGN, cap - width)
    return pl.multiple_of(start, ROUTE_ALIGN)


def _tile_slots(r0_ref, e, blk, n_blk, cap):
    r0 = r0_ref[e, blk]
    nxt = r0_ref[e, jnp.minimum(blk + 1, n_blk - 1)]
    return r0, jnp.where(blk + 1 < n_blk, nxt, cap)


def _fits_small(r0_ref, experts, blk, n_blk, cap, width):
    starts, fits = [], None
    for e in experts:
        r0, r_end = _tile_slots(r0_ref, e, blk, n_blk, cap)
        start = _window_start(r0, cap, width)
        ok = r_end - start <= width
        starts.append(start)
        fits = ok if fits is None else fits & ok
    return starts, fits


def _one_hot(pick):
    return jnp.where(pick, 1.0, 0.0).astype(BF16)


def _split3(g):
    hi = g.astype(BF16).astype(F32)
    mid = (g - hi).astype(BF16).astype(F32)
    return hi, mid, (g - hi - mid).astype(BF16).astype(F32)


def _dispatch_kernel(r0_ref, x_ref, slot_ref, gate_ref, xs_ref, *, cap, n_blk):
    ep = pl.program_id(0)
    j = pl.program_id(1)
    n_sub = x_ref.shape[0] // ROUTE_TILE
    n_exp = xs_ref.shape[0]
    experts = [ep * n_exp + ee for ee in range(n_exp)]

    @pl.when(j == 0)
    def _():
        xs_ref[...] = jnp.zeros_like(xs_ref)

    def scatter(ee, tok, start, width, pick, rows):
        gate = jnp.sum(jnp.where(pick, gate_ref[ee:ee + 1, tok], 0.0), axis=1, keepdims=True)
        hi, mid, lo = _split3(gate)
        lane = lax.broadcasted_iota(I32, (width, GATE_LANES), 1)
        cols = jnp.where(lane == 0, hi, jnp.where(lane == 1, mid, jnp.where(lane == 2, lo, 0.0)))
        win = pl.ds(start, width)
        xs_ref[ee, win, :D_MODEL] = xs_ref[ee, win, :D_MODEL] + rows.astype(BF16)
        xs_ref[ee, win, D_MODEL:] = xs_ref[ee, win, D_MODEL:] + cols.astype(BF16)

    for sb in range(n_sub):
        blk = j * n_sub + sb
        tok = slice(sb * ROUTE_TILE, (sb + 1) * ROUTE_TILE)
        starts, fits = _fits_small(r0_ref, experts, blk, n_blk, cap, DISPATCH_SMALL_WIN)

        @pl.when(fits)
        def _():
            row = lax.broadcasted_iota(I32, (DISPATCH_SMALL_WIN, ROUTE_TILE), 0)
            picks = [row == slot_ref[ee:ee + 1, tok] - starts[ee] for ee in range(n_exp)]
            rows = jnp.dot(jnp.concatenate([_one_hot(p) for p in picks], axis=0), x_ref[tok, :],
                           preferred_element_type=F32)
            for ee in range(n_exp):
                scatter(ee, tok, starts[ee], DISPATCH_SMALL_WIN, picks[ee],
                        rows[ee * DISPATCH_SMALL_WIN:(ee + 1) * DISPATCH_SMALL_WIN])

        @pl.when(jnp.logical_not(fits))
        def _():
            row = lax.broadcasted_iota(I32, (ROUTE_WIN, ROUTE_TILE), 0)
            for ee in range(n_exp):
                start = _window_start(r0_ref[experts[ee], blk], cap, ROUTE_WIN)
                pick = row == slot_ref[ee:ee + 1, tok] - start
                scatter(ee, tok, start, ROUTE_WIN, pick,
                        jnp.dot(_one_hot(pick), x_ref[tok, :], preferred_element_type=F32))


def _dispatch(xb, slot_t, aff_t, r0, cap):
    n, d = xb.shape
    n_blk = n // ROUTE_TILE
    tile = min(DISPATCH_TOKENS, n)
    n_exp = min(N_EXPERTS, DISPATCH_VMEM_BYTES // (cap * (d + GATE_LANES) * 2))
    n_exp = 1 << (n_exp.bit_length() - 1)
    pairs = lambda a: a.reshape(N_EXPERTS // n_exp, n_exp, n)
    pair_spec = pl.BlockSpec((None, n_exp, tile), lambda e, j, r0: (e, 0, j))
    grid_spec = pltpu.PrefetchScalarGridSpec(
        num_scalar_prefetch=1,
        grid=(N_EXPERTS // n_exp, n // tile),
        in_specs=[pl.BlockSpec((tile, d), lambda e, j, r0: (j, 0)), pair_spec, pair_spec],
        out_specs=pl.BlockSpec((n_exp, cap, d + GATE_LANES), lambda e, j, r0: (e, 0, 0),
                               pipeline_mode=pl.Buffered(1)),
    )
    return pl.pallas_call(
        functools.partial(_dispatch_kernel, cap=cap, n_blk=n_blk),
        grid_spec=grid_spec,
        out_shape=jax.ShapeDtypeStruct((N_EXPERTS, cap, d + GATE_LANES), BF16),
        compiler_params=_cparams("parallel", "arbitrary"),
        name="moe_dispatch",
    )(r0, xb, pairs(slot_t), pairs(aff_t))


def _ffn_kernel(xs_ref, w1_ref, w3_ref, w2_ref, o_ref):
    xs = xs_ref[:, :D_MODEL]
    gate = jnp.sum(xs_ref[:, D_MODEL:].astype(F32), axis=1, keepdims=True)
    acc = jnp.zeros((xs.shape[0], D_MODEL), F32)
    for c in range(EXPERT_FF // FF_CHUNK):
        cols = slice(c * FF_CHUNK, (c + 1) * FF_CHUNK)
        h1 = jnp.dot(xs, w1_ref[:, cols].astype(BF16), preferred_element_type=F32)
        h3 = jnp.dot(xs, w3_ref[:, cols].astype(BF16), preferred_element_type=F32)
        h = (jax.nn.silu(h1) * h3).astype(BF16)
        acc = acc + jnp.dot(h, w2_ref[cols, :].astype(BF16), preferred_element_type=F32)
    o_ref[...] = (acc * gate).astype(o_ref.dtype)


def _expert_ffn(xs, w1, w3, w2, layer):
    e, cap, width = xs.shape
    d = D_MODEL
    tile = min(FFN_TILE, cap)
    return pl.pallas_call(
        _ffn_kernel,
        grid=(e, cap // tile),
        in_specs=[pl.BlockSpec((None, tile, width), lambda i, c: (i, c, 0)),
                  pl.BlockSpec((None, None, d, EXPERT_FF), lambda i, c: (layer, i, 0, 0)),
                  pl.BlockSpec((None, None, d, EXPERT_FF), lambda i, c: (layer, i, 0, 0)),
                  pl.BlockSpec((None, None, EXPERT_FF, d), lambda i, c: (layer, i, 0, 0))],
        out_specs=pl.BlockSpec((None, tile, d), lambda i, c: (i, c, 0)),
        out_shape=jax.ShapeDtypeStruct((e, cap, d), BF16),
        compiler_params=_cparams("parallel", "parallel"),
        name="moe_ffn",
    )(xs, w1, w3, w2)


def _combine_kernel(r0_ref, x_ref, slot_ref, lg_ref, lb_ref, ye_ref, o_ref,
                    win_ref, big_ref, acc_ref, sem, big_sem, *, cap):
    j = pl.program_id(0)
    n_blk = pl.num_programs(0)
    buf = j % 2
    experts = range(N_EXPERTS)

    def small_copy(start, b, e):
        return pltpu.make_async_copy(ye_ref.at[e, pl.ds(start, ROUTE_SMALL_WIN)],
                                     win_ref.at[b, pl.ds(e * ROUTE_SMALL_WIN, ROUTE_SMALL_WIN)],
                                     sem.at[b, e])

    starts, fits = _fits_small(r0_ref, experts, j, n_blk, cap, ROUTE_SMALL_WIN)
    nxt = jnp.minimum(j + 1, n_blk - 1)
    nxt_starts, nxt_fits = _fits_small(r0_ref, experts, nxt, n_blk, cap, ROUTE_SMALL_WIN)

    @pl.when((j == 0) & fits)
    def _():
        for e in experts:
            small_copy(starts[e], 0, e).start()

    @pl.when((j + 1 < n_blk) & nxt_fits)
    def _():
        for e in experts:
            small_copy(nxt_starts[e], 1 - buf, e).start()

    slot = slot_ref[...]

    def pick(e, start, width):
        lane = lax.broadcasted_iota(I32, (ROUTE_TILE, width), 1)
        return _one_hot(lane == slot[:, e:e + 1] - start)

    @pl.when(fits)
    def _():
        for e in experts:
            small_copy(starts[e], buf, e).wait()
        cols = []
        for e in experts:
            rel = slot[:, e:e + 1] - starts[e]
            cols.append(jnp.where((rel >= 0) & (rel < ROUTE_SMALL_WIN), rel + e * ROUTE_SMALL_WIN, -1))
        tiles = []
        for t in range(N_EXPERTS * ROUTE_SMALL_WIN // 128):
            lane = lax.broadcasted_iota(I32, (ROUTE_TILE, 128), 1) + t * 128
            hit = None
            for e in experts:
                if e * ROUTE_SMALL_WIN < (t + 1) * 128 and (e + 1) * ROUTE_SMALL_WIN > t * 128:
                    hit = lane == cols[e] if hit is None else hit | (lane == cols[e])
            tiles.append(_one_hot(hit))
        acc_ref[...] = jnp.dot(jnp.concatenate(tiles, axis=1), win_ref[buf], preferred_element_type=F32)

    @pl.when(jnp.logical_not(fits))
    def _():
        acc = jnp.zeros((ROUTE_TILE, D_MODEL), F32)
        for e in experts:
            start = _window_start(r0_ref[e, j], cap, ROUTE_WIN)
            copy = pltpu.make_async_copy(ye_ref.at[e, pl.ds(start, ROUTE_WIN)], big_ref, big_sem)
            copy.start()
            copy.wait()
            acc = acc + jnp.dot(pick(e, start, ROUTE_WIN), big_ref[...], preferred_element_type=F32)
        acc_ref[...] = acc

    o_ref[...] = _layer_norm(DN_ALPHA * x_ref[...] + acc_ref[...], lg_ref[...], lb_ref[...])


def _combine(x, slot, ye, r0, ln_g, ln_b, cap):
    n, d = x.shape
    n_blk = n // ROUTE_TILE
    grid_spec = pltpu.PrefetchScalarGridSpec(
        num_scalar_prefetch=1,
        grid=(n_blk,),
        in_specs=[pl.BlockSpec((ROUTE_TILE, d), lambda j, r0: (j, 0)),
                  pl.BlockSpec((ROUTE_TILE, N_EXPERTS), lambda j, r0: (j, 0)),
                  pl.BlockSpec((1, d), lambda j, r0: (0, 0)),
                  pl.BlockSpec((1, d), lambda j, r0: (0, 0)),
                  pl.BlockSpec(memory_space=pl.ANY)],
        out_specs=pl.BlockSpec((ROUTE_TILE, d), lambda j, r0: (j, 0)),
        scratch_shapes=[pltpu.VMEM((2, N_EXPERTS * ROUTE_SMALL_WIN, d), BF16),
                        pltpu.VMEM((ROUTE_WIN, d), BF16),
                        pltpu.VMEM((ROUTE_TILE, d), F32),
                        pltpu.SemaphoreType.DMA((2, N_EXPERTS)),
                        pltpu.SemaphoreType.DMA(())],
    )
    return pl.pallas_call(
        functools.partial(_combine_kernel, cap=cap),
        grid_spec=grid_spec,
        out_shape=jax.ShapeDtypeStruct((n, d), F32),
        compiler_params=_cparams("arbitrary"),
        name="moe_combine",
    )(r0, x, slot, ln_g.reshape(1, d), ln_b.reshape(1, d), ye)


def _moe_layer(x, xb, aff_t, w1, w3, w2, layer, ln_g, ln_b):
    bsz, seq, d = x.shape
    n = bsz * seq
    cap = EC_CAPACITY_FACTOR * n // N_EXPERTS
    slot_t, r0 = _route(aff_t, cap)
    xs = _dispatch(xb.reshape(n, d), slot_t, aff_t, r0, cap)
    ye = _expert_ffn(xs, w1, w3, w2, layer)
    out = _combine(x.reshape(n, d), slot_t.T, ye, r0, ln_g, ln_b, cap)
    return out.reshape(bsz, seq, d)


def _trunk(x, mem, p):
    bsz = x.shape[0]
    m = mem.shape[1]
    for i in range(DEPTH):
        if i % 2 == 0:
            x = _s5_layer(x, p["a_w_in"][i // 2], p["s5_ops"][i // 2], p["a_w_glu"][i // 2],
                          p["ln_g"][i, 0], p["ln_b"][i, 0])
        else:
            x = _na_layer(x, p["b_w_qkv"][i // 2], p["na_bias"][i // 2], p["b_w_o"][i // 2],
                          p["ln_g"][i, 0], p["ln_b"][i, 0])
        kv = _matmul(mem.reshape(bsz * m, D_MODEL), p["m_w_kv"][i], BF16).reshape(bsz, m, 2 * D_MODEL)
        x, xb, aff_t = _xattn_router(x, kv[:, :, :D_MODEL], kv[:, :, D_MODEL:], p["m_w_q"][i], p["m_w_o"][i],
                                     p["ln_g"][i, 1], p["ln_b"][i, 1], p["e_w_router_t"][i])
        x = _moe_layer(x, xb, aff_t, p["e_w1"], p["e_w3"], p["e_w2"], i,
                       p["ln_g"][i, 2], p["ln_b"][i, 2])
    return x


def kernel(x_prompt, x_sample, mem_prompt, mem_sample, a_w_in, a_lam_re, a_lam_im, a_log_dt, a_b_re, a_b_im, a_c_re, a_c_im, a_d, a_w_glu, b_w_qkv, b_rpb, b_w_o, m_w_q, m_w_kv, m_w_o, e_w_router, e_w1, e_w3, e_w2, ln_g, ln_b):
    bf = lambda w: w.astype(BF16)
    p = {
        "a_w_in": bf(a_w_in), "a_w_glu": bf(a_w_glu),
        "s5_ops": [_s5_operators(a_lam_re[j], a_lam_im[j], a_log_dt[j], a_b_re[j], a_b_im[j],
                                 a_c_re[j], a_c_im[j], a_d[j]) for j in range(a_w_in.shape[0])],
        "b_w_qkv": bf(b_w_qkv), "b_w_o": bf(b_w_o),
        "na_bias": [_na_bias_table(b_rpb[j]) for j in range(b_rpb.shape[0])],
        "m_w_q": bf(m_w_q), "m_w_kv": bf(m_w_kv), "m_w_o": bf(m_w_o),
        "e_w_router_t": bf(jnp.swapaxes(e_w_router, 1, 2)),
        "e_w1": e_w1, "e_w3": e_w3, "e_w2": e_w2,
        "ln_g": ln_g.astype(F32), "ln_b": ln_b.astype(F32),
    }
    return (_trunk(x_prompt, mem_prompt, p), _trunk(x_sample, mem_sample, p))
```

```python
import functools

import jax
import jax.numpy as jnp
from jax import lax
from jax.experimental import pallas as pl
from jax.experimental.pallas import tpu as pltpu

F32 = jnp.float32
BF16 = jnp.bfloat16
I32 = jnp.int32

D_MODEL = 1024
DEPTH = 2
GRID_W = 64
S5_GROUP = 16
S5_GROUPS = D_MODEL // S5_GROUP
S5_STATE = 64
S5_CHUNK = 16
S5_GROUPS_PER_STEP = 4
NA_HEADS = 16
NA_HEAD_DIM = D_MODEL // NA_HEADS
NA_WIN_ROWS = 8
NA_WIN_COLS = 16
NA_QCOLS = 16
NA_KCOLS = 32
NA_ROW_BLOCK = 8
NA_KEY_ROWS = 16
NA_MASKED = -1e30
MEM_HEADS = 4
MEM_HEAD_DIM = D_MODEL // MEM_HEADS
N_EXPERTS = 16
EXPERT_FF = 2048
EC_CAPACITY_FACTOR = 2
ROUTE_TILE = 256
ROUTE_ALIGN = 16
ROUTE_WIN = ROUTE_TILE + ROUTE_ALIGN
ROUTE_SMALL_WIN = 80
DISPATCH_SMALL_WIN = 64
DISPATCH_VMEM_BYTES = 40 * 1024 * 1024
DISPATCH_TOKENS = 2048
GATE_LANES = 128
FFN_TILE = 512
FF_CHUNK = 512
DN_ALPHA = (2.0 * DEPTH) ** 0.25
LN_EPS = 1e-5

ROW_TILE = 512
XATTN_TILE = 1024
VMEM_LIMIT = 56 * 1024 * 1024


def _cparams(*sem):
    return pltpu.CompilerParams(dimension_semantics=sem, vmem_limit_bytes=VMEM_LIMIT)


def _layer_norm(v, g, b):
    mu = jnp.mean(v, axis=-1, keepdims=True)
    c = v - mu
    var = jnp.mean(c * c, axis=-1, keepdims=True)
    return c * lax.rsqrt(var + LN_EPS) * g + b


def _matmul_kernel(x_ref, w_ref, o_ref):
    o_ref[...] = jnp.dot(x_ref[...].astype(BF16), w_ref[...],
                         preferred_element_type=F32).astype(o_ref.dtype)


def _matmul(x, w, out_dtype, tile=ROW_TILE):
    n, k = x.shape
    m = w.shape[1]
    tile = min(tile, n)
    return pl.pallas_call(
        _matmul_kernel,
        grid=(n // tile,),
        in_specs=[pl.BlockSpec((tile, k), lambda i: (i, 0)),
                  pl.BlockSpec((k, m), lambda i: (0, 0))],
        out_specs=pl.BlockSpec((tile, m), lambda i: (i, 0)),
        out_shape=jax.ShapeDtypeStruct((n, m), out_dtype),
        compiler_params=_cparams("parallel"),
        name="matmul",
    )(x, w)


def _s5_ops_kernel(pw_re_ref, pw_im_ref, pws_re_ref, pws_im_ref, bt_re_ref, bt_im_ref,
                   btb_re_ref, btb_im_ref, ct_re_ref, ct_im_ref, t_ref, b_ref, c_ref):
    n = S5_CHUNK
    w = n * S5_GROUP
    exact = lax.Precision.HIGHEST
    lane = lax.broadcasted_iota(I32, (S5_STATE, w), 1)
    k_blk = lane // S5_GROUP
    c_idx = lane % S5_GROUP
    t_lane = lax.broadcasted_iota(I32, (S5_GROUP, w), 1)

    def spread_k(pw, k_of_lane):
        out = jnp.zeros((S5_STATE, w), F32)
        for k in range(n + 1):
            out = jnp.where(k_of_lane == k, pw[:, k:k + 1], out)
        return out

    def spread_c(ct):
        out = jnp.zeros((S5_STATE, w), F32)
        for c in range(S5_GROUP):
            out = jnp.where(c_idx == c, ct[:, c:c + 1], out)
        return out

    for gi in range(t_ref.shape[0]):
        kern = []
        for d in range(2):
            pw_re, pw_im = pw_re_ref[d, gi], pw_im_ref[d, gi]
            c_re, c_im = spread_c(ct_re_ref[d, gi]), spread_c(ct_im_ref[d, gi])
            lag = k_blk if d == 0 else (n - 1) - k_blk
            out = k_blk + 1 if d == 0 else n - k_blk
            a_re, a_im = spread_k(pw_re, lag), spread_k(pw_im, lag)
            cp_re, cp_im = c_re * a_re - c_im * a_im, c_re * a_im + c_im * a_re
            kern.append(jnp.dot(bt_re_ref[d, gi], cp_re, precision=exact, preferred_element_type=F32)
                        - jnp.dot(bt_im_ref[d, gi], cp_im, precision=exact, preferred_element_type=F32))
            a_re, a_im = spread_k(pw_re, out), spread_k(pw_im, out)
            c_ref[gi, d * S5_STATE:(d + 1) * S5_STATE, :] = (c_re * a_re - c_im * a_im).astype(BF16)
            c_ref[gi, (2 + d) * S5_STATE:(3 + d) * S5_STATE, :] = (-(c_re * a_im + c_im * a_re)).astype(BF16)
        kern_f, kern_b = kern
        bt_re, bt_im = btb_re_ref[gi], btb_im_ref[gi]
        for s in range(n):
            rows = slice(s * S5_GROUP, (s + 1) * S5_GROUP)
            fwd = pltpu.roll(kern_f, s * S5_GROUP, 1) if s else kern_f
            fwd = jnp.where(t_lane >= s * S5_GROUP, fwd, 0.0)
            shift = ((s + 1) * S5_GROUP) % w
            bwd = pltpu.roll(kern_b, shift, 1) if shift else kern_b
            bwd = jnp.where(t_lane < (s + 1) * S5_GROUP, bwd, 0.0)
            t_ref[gi, rows, :] = (fwd + bwd).astype(BF16)
            p_re, p_im = pws_re_ref[gi, s:s + 1, :], pws_im_ref[gi, s:s + 1, :]
            b_ref[gi, rows, :2 * S5_STATE] = (p_re * bt_re - p_im * bt_im).astype(BF16)
            b_ref[gi, rows, 2 * S5_STATE:] = (p_re * bt_im + p_im * bt_re).astype(BF16)


def _s5_operators(lam_re, lam_im, log_dt, b_re, b_im, c_re, c_im, d_skip):
    n = S5_CHUNK
    dt = jnp.exp(log_dt.astype(F32))[..., None]
    lr = lam_re.astype(F32) * dt
    li = lam_im.astype(F32) * dt
    k = jnp.arange(n + 1, dtype=F32)
    mag = jnp.exp(lr[..., None] * k)
    pw_re = mag * jnp.cos(li[..., None] * k)
    pw_im = mag * jnp.sin(li[..., None] * k)
    x = pw_re[..., 1] - 1.0
    y = pw_im[..., 1]
    a = lam_re.astype(F32)
    b = lam_im.astype(F32)
    den = a * a + b * b
    q_re = ((x * a + y * b) / den)[..., None]
    q_im = ((y * a - x * b) / den)[..., None]
    bb_re = q_re * b_re.astype(F32) - q_im * b_im.astype(F32)
    bb_im = q_re * b_im.astype(F32) + q_im * b_re.astype(F32)
    bt_re = jnp.swapaxes(bb_re, 2, 3)
    bt_im = jnp.swapaxes(bb_im, 2, 3)
    ct_re = jnp.swapaxes(c_re.astype(F32), 2, 3)
    ct_im = jnp.swapaxes(c_im.astype(F32), 2, 3)
    both = lambda fwd, bwd: jnp.concatenate([fwd, bwd], axis=-1)
    pwt_re = jnp.swapaxes(pw_re[..., :n], 2, 3)
    pwt_im = jnp.swapaxes(pw_im[..., :n], 2, 3)
    pws_re = both(jnp.flip(pwt_re[0], axis=1), pwt_re[1])
    pws_im = both(jnp.flip(pwt_im[0], axis=1), pwt_im[1])
    gb = S5_GROUPS_PER_STEP
    per_dir = lambda shape: pl.BlockSpec((2, gb) + shape, lambda i: (0, i, 0, 0))
    per_grp = lambda shape: pl.BlockSpec((gb,) + shape, lambda i: (i, 0, 0))
    w = n * S5_GROUP
    t_op, b_op, c_op = pl.pallas_call(
        _s5_ops_kernel,
        grid=(S5_GROUPS // gb,),
        in_specs=[per_dir((S5_STATE, n + 1)), per_dir((S5_STATE, n + 1)),
                  per_grp((n, 2 * S5_STATE)), per_grp((n, 2 * S5_STATE)),
                  per_dir((S5_GROUP, S5_STATE)), per_dir((S5_GROUP, S5_STATE)),
                  per_grp((S5_GROUP, 2 * S5_STATE)), per_grp((S5_GROUP, 2 * S5_STATE)),
                  per_dir((S5_STATE, S5_GROUP)), per_dir((S5_STATE, S5_GROUP))],
        out_specs=[per_grp((w, w))] * 3,
        out_shape=[jax.ShapeDtypeStruct((S5_GROUPS, w, w), BF16)] * 3,
        compiler_params=_cparams("parallel"),
        name="s5_operators",
    )(pw_re, pw_im, pws_re, pws_im, bt_re, bt_im, both(bt_re[0], bt_re[1]), both(bt_im[0], bt_im[1]),
      ct_re, ct_im)
    a_re = jnp.concatenate([pw_re[0][..., n], pw_re[1][..., n]], axis=-1)
    a_im = jnp.concatenate([pw_im[0][..., n], pw_im[1][..., n]], axis=-1)
    a_op = jnp.stack([a_re, a_im], axis=1)
    a_op = jnp.broadcast_to(a_op[:, :, None, :], (S5_GROUPS, 2, 8, 128))
    d_op = jnp.tile(d_skip.astype(F32).reshape(S5_GROUPS, 1, 16), (1, 1, n))
    return t_op, b_op, c_op, a_op, d_op


def _s5_kernel(u_ref, t_ref, b_ref, c_ref, a_ref, d_ref, o_ref, s_scr, hc_scr, *, n_tiles):
    gb = u_ref.shape[0]
    for gi in range(gb):
        s_scr[gi] = jnp.dot(u_ref[gi].astype(BF16), b_ref[gi], preferred_element_type=F32)

    row = lax.broadcasted_iota(I32, (8, 128), 0)
    lane = lax.broadcasted_iota(I32, (8, 128), 1)
    low_rows = row < 4
    fwd_lanes = lane < S5_STATE
    a_re = [a_ref[gi, 0] for gi in range(gb)]
    a_im = [a_ref[gi, 1] for gi in range(gb)]

    def cmul_add(gi, h_re, h_im, v_re, v_im):
        return (a_re[gi] * h_re - a_im[gi] * h_im + v_re,
                a_re[gi] * h_im + a_im[gi] * h_re + v_im)

    def half_steps(gi, k, c_re, c_im):
        rows = pl.ds(pl.multiple_of(k * 8, 8), 8)
        v_re = s_scr[gi, rows, 0:128]
        v_im = s_scr[gi, rows, 128:256]
        h1_re, h1_im = cmul_add(gi, c_re, c_im, v_re, v_im)
        h1s_re = pltpu.roll(h1_re, 4, 0)
        h1s_im = pltpu.roll(h1_im, 4, 0)
        h2_re, h2_im = cmul_add(gi, h1s_re, h1s_im, v_re, v_im)
        return rows, h1s_re, h1s_im, pltpu.roll(h2_re, 4, 0), pltpu.roll(h2_im, 4, 0)

    def fwd_body(k, carry):
        out = []
        for gi in range(gb):
            c_re, c_im = carry[2 * gi], carry[2 * gi + 1]
            rows, h1s_re, h1s_im, n_re, n_im = half_steps(gi, k, c_re, c_im)
            hc_scr[gi, rows, 0:128] = jnp.where(low_rows, c_re, h1s_re)
            hc_scr[gi, rows, 128:256] = jnp.where(low_rows, c_im, h1s_im)
            out += [n_re, n_im]
        return tuple(out)

    def bwd_body(i, carry):
        k = n_tiles - 1 - i
        out = []
        for gi in range(gb):
            c_re, c_im = carry[2 * gi], carry[2 * gi + 1]
            rows, h1s_re, h1s_im, n_re, n_im = half_steps(gi, k, c_re, c_im)
            new_re = jnp.where(low_rows, h1s_re, c_re)
            new_im = jnp.where(low_rows, h1s_im, c_im)
            hc_scr[gi, rows, 0:128] = jnp.where(fwd_lanes, hc_scr[gi, rows, 0:128], new_re)
            hc_scr[gi, rows, 128:256] = jnp.where(fwd_lanes, hc_scr[gi, rows, 128:256], new_im)
            out += [n_re, n_im]
        return tuple(out)

    zeros = tuple(jnp.zeros((8, 128), F32) for _ in range(2 * gb))
    lax.fori_loop(0, n_tiles, fwd_body, zeros)
    lax.fori_loop(0, n_tiles, bwd_body, zeros)

    for gi in range(gb):
        u = u_ref[gi]
        y = (jnp.dot(u.astype(BF16), t_ref[gi], preferred_element_type=F32)
             + jnp.dot(hc_scr[gi].astype(BF16), c_ref[gi], preferred_element_type=F32)
             + d_ref[gi] * u)
        o_ref[gi] = jax.nn.gelu(y).astype(o_ref.dtype)


def _s5_scan(u_rows, ops):
    t_op, b_op, c_op, a_op, d_op = ops
    g, m, w = u_rows.shape
    gb = S5_GROUPS_PER_STEP
    spec3 = lambda shape: pl.BlockSpec((gb,) + shape, lambda i: (i,) + (0,) * len(shape))
    return pl.pallas_call(
        functools.partial(_s5_kernel, n_tiles=m // 8),
        grid=(g // gb,),
        in_specs=[spec3((m, w)), spec3((w, w)), spec3((w, w)), spec3((w, w)),
                  spec3((2, 8, 128)), spec3((1, w))],
        out_specs=spec3((m, w)),
        out_shape=jax.ShapeDtypeStruct((g, m, w), F32),
        scratch_shapes=[pltpu.VMEM((gb, m, w), F32), pltpu.VMEM((gb, m, w), F32)],
        compiler_params=_cparams("parallel"),
        name="s5_scan",
    )(u_rows, t_op, b_op, c_op, a_op, d_op)


def _proj_ln_kernel(h_ref, x_ref, w_ref, lg_ref, lb_ref, o_ref):
    h = jnp.dot(h_ref[...], w_ref[...], preferred_element_type=F32)
    o_ref[...] = _layer_norm(DN_ALPHA * x_ref[...] + h, lg_ref[...], lb_ref[...])


def _residual_ln(body, h, x, w, ln_g, ln_b, name):
    n, d = x.shape
    tile = min(ROW_TILE, n)
    row = lambda width: pl.BlockSpec((tile, width), lambda i: (i, 0))
    const = lambda shape: pl.BlockSpec(shape, lambda i: (0, 0))
    return pl.pallas_call(
        body,
        grid=(n // tile,),
        in_specs=[row(h.shape[1]), row(d), const(w.shape), const((1, d)), const((1, d))],
        out_specs=row(d),
        out_shape=jax.ShapeDtypeStruct((n, d), F32),
        compiler_params=_cparams("parallel"),
        name=name,
    )(h, x, w, ln_g.reshape(1, d), ln_b.reshape(1, d))


def _block_transpose(x):
    rows, lanes = x.ndim - 2, x.ndim - 1
    row = lax.broadcasted_iota(I32, x.shape, rows)
    blk = lax.broadcasted_iota(I32, x.shape, lanes) // S5_GROUP
    for d in (1, 2, 4):
        up = pltpu.roll(pltpu.roll(x, 8 - d, rows), S5_GROUP * d, lanes)
        down = pltpu.roll(pltpu.roll(x, d, rows), 128 - S5_GROUP * d, lanes)
        col_bit = (blk & d) != 0
        x = jnp.where((row & d) == (blk & d), x, jnp.where(col_bit, up, down))
    return x


def _s5_in_kernel(x_ref, w_ref, o_ref):
    bsz, steps, d = x_ref.shape
    n_chunk = steps // S5_CHUNK
    u = jnp.dot(x_ref[...].reshape(bsz * steps, d).astype(BF16), w_ref[...], preferred_element_type=F32)
    u = u.reshape(bsz, n_chunk, S5_CHUNK, d)
    for s in range(S5_CHUNK // 8):
        for c in range(d // 128):
            w = _block_transpose(u[:, :, s * 8:(s + 1) * 8, c * 128:(c + 1) * 128])
            for jj in range(n_chunk):
                for b in range(bsz):
                    o_ref[c * 8:(c + 1) * 8, jj * bsz + b, s * 128:(s + 1) * 128] = w[b, jj]


def _s5_glu_ln_kernel(g_ref, x_ref, w_ref, lg_ref, lb_ref, o_ref, tok_scr):
    bsz, steps, d = x_ref.shape
    n_chunk = steps // S5_CHUNK
    for s in range(S5_CHUNK // 8):
        for c in range(d // 128):
            w = jnp.stack([jnp.stack([g_ref[c * 8:(c + 1) * 8, jj * bsz + b, s * 128:(s + 1) * 128]
                                      for jj in range(n_chunk)]) for b in range(bsz)])
            tok_scr[:, :, s * 8:(s + 1) * 8, c * 128:(c + 1) * 128] = _block_transpose(w)
    g = tok_scr[...].reshape(bsz * steps, d).astype(BF16)
    z = jnp.dot(g, w_ref[...], preferred_element_type=F32)
    h = z[:, :d] * jax.nn.sigmoid(z[:, d:])
    x = x_ref[...].reshape(bsz * steps, d)
    o_ref[...] = _layer_norm(DN_ALPHA * x + h, lg_ref[...], lb_ref[...]).reshape(bsz, steps, d)


def _s5_layer(x, w_in, ops, w_glu, ln_g, ln_b):
    bsz, seq, d = x.shape
    assert bsz == 4, "the chunk-state scan packs two chunk steps x four batches into one 8-row tile"
    j = seq // S5_CHUNK
    steps = ROW_TILE // bsz
    n_chunk = steps // S5_CHUNK
    width = S5_CHUNK * S5_GROUP
    tok_spec = pl.BlockSpec((bsz, steps, d), lambda i: (0, i, 0))
    row_spec = pl.BlockSpec((S5_GROUPS, n_chunk * bsz, width), lambda i: (0, i, 0))
    const = lambda shape: pl.BlockSpec(shape, lambda i: (0, 0))
    u_rows = pl.pallas_call(
        _s5_in_kernel,
        grid=(seq // steps,),
        in_specs=[tok_spec, const((d, d))],
        out_specs=row_spec,
        out_shape=jax.ShapeDtypeStruct((S5_GROUPS, j * bsz, width), F32),
        compiler_params=_cparams("parallel"),
        name="s5_in_proj",
    )(x, w_in)
    g_rows = _s5_scan(u_rows, ops)
    return pl.pallas_call(
        _s5_glu_ln_kernel,
        grid=(seq // steps,),
        in_specs=[row_spec, tok_spec, const((d, 2 * d)), const((1, d)), const((1, d))],
        out_specs=tok_spec,
        out_shape=jax.ShapeDtypeStruct((bsz, seq, d), F32),
        scratch_shapes=[pltpu.VMEM((bsz, n_chunk, S5_CHUNK, d), F32)],
        compiler_params=_cparams("parallel"),
        name="s5_glu_ln",
    )(g_rows, x, w_glu, ln_g.reshape(1, d), ln_b.reshape(1, d))


def _na_key_col_starts():
    n_cb = GRID_W // NA_QCOLS
    return [min(max(n * NA_QCOLS - NA_WIN_COLS // 2, 0), GRID_W - NA_KCOLS) for n in range(n_cb)]


def _na_bias_table(rpb):
    exact = lax.Precision.HIGHEST
    n_cb = GRID_W // NA_QCOLS
    starts = jnp.asarray(_na_key_col_starts(), I32)[:, None, None]
    half = NA_WIN_ROWS // 2
    rl = jnp.arange(NA_ROW_BLOCK)[:, None]
    kl = jnp.arange(NA_KEY_ROWS)[None, :]
    dr = jnp.clip(kl - rl + (NA_WIN_ROWS - 1 - half), 0, 2 * NA_WIN_ROWS - 2)
    qc = jnp.arange(n_cb)[:, None, None] * NA_QCOLS + jnp.arange(NA_QCOLS)[None, :, None]
    kc = starts + jnp.arange(NA_KCOLS)[None, None, :]
    dc = jnp.clip(kc - qc + NA_WIN_COLS - 1, 0, 2 * NA_WIN_COLS - 2)
    pick_dc = (dc[..., None] == jnp.arange(2 * NA_WIN_COLS - 1)).astype(F32)
    pick_dr = (dr[..., None] == jnp.arange(2 * NA_WIN_ROWS - 1)).astype(F32)
    by_col = jnp.einsum('hrc,nqkc->hrnqk', rpb.astype(F32), pick_dc, precision=exact)
    bias = jnp.einsum('hrnqk,alr->nhaqlk', by_col, pick_dr, precision=exact)
    nq = NA_ROW_BLOCK * NA_QCOLS
    nk = NA_KEY_ROWS * NA_KCOLS
    bias = bias.reshape(n_cb, 1, NA_HEADS // 2, 2 * nq, nk)
    row = jnp.arange(2 * nq)[:, None]
    col = jnp.arange(nk)[None, :]
    f_rl, f_qcl = (row % nq) // NA_QCOLS, row % NA_QCOLS
    f_kl, f_kcl = col // NA_KCOLS, col % NA_KCOLS
    f_rs = jnp.stack([jnp.maximum(f_rl - half, 0), f_rl - half, jnp.minimum(f_rl - half, 0)])
    f_row_ok = (f_kl - half >= f_rs) & (f_kl - half < f_rs + NA_WIN_ROWS)
    blk = jnp.arange(n_cb)[:, None, None]
    f_qc = blk * NA_QCOLS + f_qcl
    f_kc = starts + f_kcl
    f_ws = jnp.clip(f_qc - NA_WIN_COLS // 2, 0, GRID_W - NA_WIN_COLS)
    f_col_ok = (f_kc >= f_ws) & (f_kc < f_ws + NA_WIN_COLS)
    ok = f_row_ok[None, :, None] & f_col_ok[:, None, None]
    return jnp.where(ok, bias, NA_MASKED)


def _qkv_kernel(x_ref, w_ref, q_ref, k_ref, v_ref):
    qkv = jnp.dot(x_ref[...].astype(BF16), w_ref[...], preferred_element_type=F32)
    q_ref[...] = (qkv[:, :D_MODEL] * (NA_HEAD_DIM ** -0.5)).astype(BF16)
    k = qkv[:, D_MODEL:2 * D_MODEL]
    v = qkv[:, 2 * D_MODEL:]
    for r in range(NA_ROW_BLOCK):
        for n, start in enumerate(_na_key_col_starts()):
            lo = r * GRID_W + start
            k_ref[r, n] = k[lo:lo + NA_KCOLS].astype(BF16)
            v_ref[r, n] = v[lo:lo + NA_KCOLS].astype(BF16)


def _na_kernel(q_ref, kp_ref, kc_ref, kn_ref, vp_ref, vc_ref, vn_ref, bias_ref, o_ref, k_scr, v_scr):
    nq = NA_ROW_BLOCK * NA_QCOLS
    quarter = 4 * NA_KCOLS
    k_scr[0:quarter] = kp_ref[...].reshape(quarter, D_MODEL)
    k_scr[quarter:3 * quarter] = kc_ref[...].reshape(2 * quarter, D_MODEL)
    k_scr[3 * quarter:] = kn_ref[...].reshape(quarter, D_MODEL)
    v_scr[0:quarter] = vp_ref[...].reshape(quarter, D_MODEL)
    v_scr[quarter:3 * quarter] = vc_ref[...].reshape(2 * quarter, D_MODEL)
    v_scr[3 * quarter:] = vn_ref[...].reshape(quarter, D_MODEL)
    q = q_ref[...].reshape(nq, D_MODEL)
    first_head = lax.broadcasted_iota(I32, (nq, 128), 1) < NA_HEAD_DIM
    zero = jnp.zeros((nq, 128), BF16)
    for hp in range(NA_HEADS // 2):
        lanes = slice(hp * 128, (hp + 1) * 128)
        q2 = q[:, lanes]
        qs = jnp.concatenate([jnp.where(first_head, q2, zero), jnp.where(first_head, zero, q2)], axis=0)
        s = lax.dot_general(qs, k_scr[:, lanes], (((1,), (1,)), ((), ())),
                            preferred_element_type=F32) + bias_ref[hp]
        m = jnp.max(s, axis=-1, keepdims=True)
        p = jnp.exp(s - m)
        l = jnp.sum(p, axis=-1, keepdims=True)
        o = jnp.dot(p.astype(BF16), v_scr[:, lanes], preferred_element_type=F32) / l
        o_ref[:, :, lanes] = jnp.where(first_head, o[:nq], o[nq:]).astype(BF16).reshape(
            NA_ROW_BLOCK, NA_QCOLS, 128)


def _na_layer(x, w_qkv, bias_table, w_o, ln_g, ln_b):
    bsz, seq, d = x.shape
    rows = seq // GRID_W
    n_cb = GRID_W // NA_QCOLS
    n_rb = rows // NA_ROW_BLOCK
    assert n_rb >= 2
    tile = NA_ROW_BLOCK * GRID_W
    q, kx, vx = pl.pallas_call(
        _qkv_kernel,
        grid=(bsz, n_rb),
        in_specs=[pl.BlockSpec((None, tile, d), lambda b, i: (b, i, 0)),
                  pl.BlockSpec((d, 3 * d), lambda b, i: (0, 0))],
        out_specs=[pl.BlockSpec((None, tile, d), lambda b, i: (b, i, 0)),
                   pl.BlockSpec((None, NA_ROW_BLOCK, n_cb, NA_KCOLS, d), lambda b, i: (b, i, 0, 0, 0)),
                   pl.BlockSpec((None, NA_ROW_BLOCK, n_cb, NA_KCOLS, d), lambda b, i: (b, i, 0, 0, 0))],
        out_shape=[jax.ShapeDtypeStruct((bsz, seq, d), BF16),
                   jax.ShapeDtypeStruct((bsz, rows, n_cb, NA_KCOLS, d), BF16),
                   jax.ShapeDtypeStruct((bsz, rows, n_cb, NA_KCOLS, d), BF16)],
        compiler_params=_cparams("parallel", "parallel"),
        name="na_qkv",
    )(x, w_qkv)

    q5 = q.reshape(bsz, rows, n_cb, NA_QCOLS, d)
    n_half = rows // 4
    half_view = lambda a: a.reshape(bsz, n_half, 4, n_cb, NA_KCOLS, d)
    full_view = lambda a: a.reshape(bsz, n_rb, NA_ROW_BLOCK, n_cb, NA_KCOLS, d)
    prev_spec = pl.BlockSpec((None, None, 4, None, NA_KCOLS, d),
                             lambda n, b, i: (b, jnp.maximum(2 * i - 1, 0), 0, n, 0, 0))
    cur_spec = pl.BlockSpec((None, None, NA_ROW_BLOCK, None, NA_KCOLS, d),
                            lambda n, b, i: (b, i, 0, n, 0, 0))
    next_spec = pl.BlockSpec((None, None, 4, None, NA_KCOLS, d),
                             lambda n, b, i: (b, jnp.minimum(2 * i + 2, n_half - 1), 0, n, 0, 0))
    kind = lambda i: jnp.where(i == 0, 0, jnp.where(i == n_rb - 1, 2, 1))
    nk = NA_KEY_ROWS * NA_KCOLS
    o5 = pl.pallas_call(
        _na_kernel,
        grid=(n_cb, bsz, n_rb),
        in_specs=[pl.BlockSpec((None, NA_ROW_BLOCK, None, NA_QCOLS, d), lambda n, b, i: (b, i, n, 0, 0)),
                  prev_spec, cur_spec, next_spec, prev_spec, cur_spec, next_spec,
                  pl.BlockSpec((None, None) + bias_table.shape[2:],
                               lambda n, b, i: (n, kind(i), 0, 0, 0))],
        out_specs=pl.BlockSpec((None, NA_ROW_BLOCK, None, NA_QCOLS, d), lambda n, b, i: (b, i, n, 0, 0)),
        out_shape=jax.ShapeDtypeStruct((bsz, rows, n_cb, NA_QCOLS, d), BF16),
        scratch_shapes=[pltpu.VMEM((nk, d), BF16), pltpu.VMEM((nk, d), BF16)],
        compiler_params=_cparams("parallel", "parallel", "parallel"),
        name="na_attn",
    )(q5, half_view(kx), full_view(kx), half_view(kx), half_view(vx), full_view(vx), half_view(vx),
      bias_table)
    o = o5.reshape(bsz * seq, d)
    return _residual_ln(_proj_ln_kernel, o, x.reshape(bsz * seq, d), w_o, ln_g, ln_b,
                        "na_out_ln").reshape(bsz, seq, d)


def _xattn_kernel(x_ref, k_ref, v_ref, wq_ref, wo_ref, lg_ref, lb_ref, wr_ref, o_ref, ob_ref, aff_ref):
    x = x_ref[...]
    q = jnp.dot(x.astype(BF16), wq_ref[...], preferred_element_type=F32).astype(BF16)
    heads = []
    for h in range(MEM_HEADS):
        lanes = slice(h * MEM_HEAD_DIM, (h + 1) * MEM_HEAD_DIM)
        s = lax.dot_general(q[:, lanes], k_ref[:, lanes], (((1,), (1,)), ((), ())),
                            preferred_element_type=F32) * (MEM_HEAD_DIM ** -0.5)
        m = jnp.max(s, axis=-1, keepdims=True)
        p = jnp.exp(s - m)
        p = p / jnp.sum(p, axis=-1, keepdims=True)
        heads.append(jnp.dot(p.astype(BF16), v_ref[:, lanes], preferred_element_type=F32).astype(BF16))
    o = jnp.concatenate(heads, axis=-1)
    r = jnp.dot(o, wo_ref[...], preferred_element_type=F32)
    y = _layer_norm(DN_ALPHA * x + r, lg_ref[...], lb_ref[...])
    o_ref[...] = y
    yb = y.astype(BF16)
    ob_ref[...] = yb
    logits = lax.dot_general(wr_ref[...], yb, (((1,), (1,)), ((), ())), preferred_element_type=F32)
    e = jnp.exp(logits - jnp.max(logits, axis=0, keepdims=True))
    aff_ref[...] = e / jnp.sum(e, axis=0, keepdims=True)


def _xattn_router(x, mem_k, mem_v, w_q, w_o, ln_g, ln_b, w_router_t):
    bsz, seq, d = x.shape
    m = mem_k.shape[1]
    tile = min(XATTN_TILE, seq)
    n_t = seq // tile
    const = lambda shape: pl.BlockSpec(shape, lambda b, i: (0, 0))
    return pl.pallas_call(
        _xattn_kernel,
        grid=(bsz, n_t),
        in_specs=[pl.BlockSpec((None, tile, d), lambda b, i: (b, i, 0)),
                  pl.BlockSpec((None, m, d), lambda b, i: (b, 0, 0)),
                  pl.BlockSpec((None, m, d), lambda b, i: (b, 0, 0)),
                  const((d, d)), const((d, d)), const((1, d)), const((1, d)), const((N_EXPERTS, d))],
        out_specs=[pl.BlockSpec((None, tile, d), lambda b, i: (b, i, 0)),
                   pl.BlockSpec((None, tile, d), lambda b, i: (b, i, 0)),
                   pl.BlockSpec((N_EXPERTS, tile), lambda b, i: (0, b * n_t + i))],
        out_shape=[jax.ShapeDtypeStruct((bsz, seq, d), F32),
                   jax.ShapeDtypeStruct((bsz, seq, d), BF16),
                   jax.ShapeDtypeStruct((N_EXPERTS, bsz * seq), F32)],
        compiler_params=_cparams("parallel", "parallel"),
        name="xattn_router",
    )(x, mem_k, mem_v, w_q, w_o, ln_g.reshape(1, d), ln_b.reshape(1, d), w_router_t)


def _route_kernel(aff_ref, slot_ref, r0_ref, *, cap):
    n = aff_ref.shape[1]
    n_blk = n // ROUTE_TILE
    aff = aff_ref[...]

    def count(mask):
        return jnp.sum(jnp.where(mask, 1.0, 0.0), axis=1, keepdims=True)

    def search(i, bits):
        cand = bits | lax.shift_left(jnp.int32(1), 30 - i)
        ge = aff >= lax.bitcast_convert_type(cand, F32)
        return jnp.where(count(ge) >= cap, cand, bits)

    tau = lax.bitcast_convert_type(
        lax.fori_loop(0, 31, search, jnp.zeros((N_EXPERTS, 1), I32)), F32)
    need = cap - count(aff > tau)
    upper = (lax.broadcasted_iota(I32, (ROUTE_TILE, ROUTE_TILE), 0)
             < lax.broadcasted_iota(I32, (ROUTE_TILE, ROUTE_TILE), 1)).astype(BF16)
    blk_lane = lax.broadcasted_iota(I32, (N_EXPERTS, n_blk), 1)

    r0_ref[...] = jnp.zeros_like(r0_ref)

    def block(kb, carry):
        c_eq, c_sel = carry
        cols = pl.ds(pl.multiple_of(kb * ROUTE_TILE, ROUTE_TILE), ROUTE_TILE)
        b = aff_ref[:, cols]
        eq = b == tau
        eq_f = jnp.where(eq, 1.0, 0.0)
        eq_rank = c_eq + jnp.dot(eq_f.astype(BF16), upper, preferred_element_type=F32)
        sel = (b > tau) | (eq & (eq_rank < need))
        sel_f = jnp.where(sel, 1.0, 0.0)
        rank = c_sel + jnp.dot(sel_f.astype(BF16), upper, preferred_element_type=F32)
        slot_ref[:, cols] = jnp.where(sel, rank.astype(I32), -1)
        r0_ref[...] = jnp.where(blk_lane == kb, c_sel.astype(I32), r0_ref[...])
        return (c_eq + jnp.sum(eq_f, axis=1, keepdims=True),
                c_sel + jnp.sum(sel_f, axis=1, keepdims=True))

    zero = jnp.zeros((N_EXPERTS, 1), F32)
    lax.fori_loop(0, n_blk, block, (zero, zero))


def _route(aff_t, cap):
    e, n = aff_t.shape
    n_blk = n // ROUTE_TILE
    return pl.pallas_call(
        functools.partial(_route_kernel, cap=cap),
        out_shape=[jax.ShapeDtypeStruct((e, n), I32), jax.ShapeDtypeStruct((e, n_blk), I32)],
        compiler_params=pltpu.CompilerParams(vmem_limit_bytes=VMEM_LIMIT),
        name="route",
    )(aff_t)


def _window_start(r0, cap, width):
    start = jnp.minimum((r0 // ROUTE_ALIGN) * ROUTE_ALIGN, cap - width)
    return pl.multiple_of(start, ROUTE_ALIGN)


def _tile_slots(r0_ref, e, blk, n_blk, cap):
    r0 = r0_ref[e, blk]
    nxt = r0_ref[e, jnp.minimum(blk + 1, n_blk - 1)]
    return r0, jnp.where(blk + 1 < n_blk, nxt, cap)


def _fits_small(r0_ref, experts, blk, n_blk, cap, width):
    starts, fits = [], None
    for e in experts:
        r0, r_end = _tile_slots(r0_ref, e, blk, n_blk, cap)
        start = _window_start(r0, cap, width)
        ok = r_end - start <= width
        starts.append(start)
        fits = ok if fits is None else fits & ok
    return starts, fits


def _one_hot(pick):
    return jnp.where(pick, 1.0, 0.0).astype(BF16)


def _split3(g):
    hi = g.astype(BF16).astype(F32)
    mid = (g - hi).astype(BF16).astype(F32)
    return hi, mid, (g - hi - mid).astype(BF16).astype(F32)


def _dispatch_kernel(r0_ref, x_ref, slot_ref, gate_ref, xs_ref, *, cap, n_blk):
    ep = pl.program_id(0)
    j = pl.program_id(1)
    n_sub = x_ref.shape[0] // ROUTE_TILE
    n_exp = xs_ref.shape[0]
    experts = [ep * n_exp + ee for ee in range(n_exp)]

    @pl.when(j == 0)
    def _():
        xs_ref[...] = jnp.zeros_like(xs_ref)

    def scatter(ee, tok, start, width, pick, rows):
        gate = jnp.sum(jnp.where(pick, gate_ref[ee:ee + 1, tok], 0.0), axis=1, keepdims=True)
        hi, mid, lo = _split3(gate)
        lane = lax.broadcasted_iota(I32, (width, GATE_LANES), 1)
        cols = jnp.where(lane == 0, hi, jnp.where(lane == 1, mid, jnp.where(lane == 2, lo, 0.0)))
        win = pl.ds(start, width)
        xs_ref[ee, win, :D_MODEL] = xs_ref[ee, win, :D_MODEL] + rows.astype(BF16)
        xs_ref[ee, win, D_MODEL:] = xs_ref[ee, win, D_MODEL:] + cols.astype(BF16)

    for sb in range(n_sub):
        blk = j * n_sub + sb
        tok = slice(sb * ROUTE_TILE, (sb + 1) * ROUTE_TILE)
        starts, fits = _fits_small(r0_ref, experts, blk, n_blk, cap, DISPATCH_SMALL_WIN)

        @pl.when(fits)
        def _():
            row = lax.broadcasted_iota(I32, (DISPATCH_SMALL_WIN, ROUTE_TILE), 0)
            picks = [row == slot_ref[ee:ee + 1, tok] - starts[ee] for ee in range(n_exp)]
            rows = jnp.dot(jnp.concatenate([_one_hot(p) for p in picks], axis=0), x_ref[tok, :],
                           preferred_element_type=F32)
            for ee in range(n_exp):
                scatter(ee, tok, starts[ee], DISPATCH_SMALL_WIN, picks[ee],
                        rows[ee * DISPATCH_SMALL_WIN:(ee + 1) * DISPATCH_SMALL_WIN])

        @pl.when(jnp.logical_not(fits))
        def _():
            row = lax.broadcasted_iota(I32, (ROUTE_WIN, ROUTE_TILE), 0)
            for ee in range(n_exp):
                start = _window_start(r0_ref[experts[ee], blk], cap, ROUTE_WIN)
                pick = row == slot_ref[ee:ee + 1, tok] - start
                scatter(ee, tok, start, ROUTE_WIN, pick,
                        jnp.dot(_one_hot(pick), x_ref[tok, :], preferred_element_type=F32))


def _dispatch(xb, slot_t, aff_t, r0, cap):
    n, d = xb.shape
    n_blk = n // ROUTE_TILE
    tile = min(DISPATCH_TOKENS, n)
    n_exp = min(N_EXPERTS, DISPATCH_VMEM_BYTES // (cap * (d + GATE_LANES) * 2))
    n_exp = 1 << (n_exp.bit_length() - 1)
    pairs = lambda a: a.reshape(N_EXPERTS // n_exp, n_exp, n)
    pair_spec = pl.BlockSpec((None, n_exp, tile), lambda e, j, r0: (e, 0, j))
    grid_spec = pltpu.PrefetchScalarGridSpec(
        num_scalar_prefetch=1,
        grid=(N_EXPERTS // n_exp, n // tile),
        in_specs=[pl.BlockSpec((tile, d), lambda e, j, r0: (j, 0)), pair_spec, pair_spec],
        out_specs=pl.BlockSpec((n_exp, cap, d + GATE_LANES), lambda e, j, r0: (e, 0, 0),
                               pipeline_mode=pl.Buffered(1)),
    )
    return pl.pallas_call(
        functools.partial(_dispatch_kernel, cap=cap, n_blk=n_blk),
        grid_spec=grid_spec,
        out_shape=jax.ShapeDtypeStruct((N_EXPERTS, cap, d + GATE_LANES), BF16),
        compiler_params=_cparams("parallel", "arbitrary"),
        name="moe_dispatch",
    )(r0, xb, pairs(slot_t), pairs(aff_t))


def _ffn_kernel(xs_ref, w1_ref, w3_ref, w2_ref, o_ref):
    xs = xs_ref[:, :D_MODEL]
    gate = jnp.sum(xs_ref[:, D_MODEL:].astype(F32), axis=1, keepdims=True)
    acc = jnp.zeros((xs.shape[0], D_MODEL), F32)
    for c in range(EXPERT_FF // FF_CHUNK):
        cols = slice(c * FF_CHUNK, (c + 1) * FF_CHUNK)
        h1 = jnp.dot(xs, w1_ref[:, cols].astype(BF16), preferred_element_type=F32)
        h3 = jnp.dot(xs, w3_ref[:, cols].astype(BF16), preferred_element_type=F32)
        h = (jax.nn.silu(h1) * h3).astype(BF16)
        acc = acc + jnp.dot(h, w2_ref[cols, :].astype(BF16), preferred_element_type=F32)
    o_ref[...] = (acc * gate).astype(o_ref.dtype)


def _expert_ffn(xs, w1, w3, w2, layer):
    e, cap, width = xs.shape
    d = D_MODEL
    tile = min(FFN_TILE, cap)
    return pl.pallas_call(
        _ffn_kernel,
        grid=(e, cap // tile),
        in_specs=[pl.BlockSpec((None, tile, width), lambda i, c: (i, c, 0)),
                  pl.BlockSpec((None, None, d, EXPERT_FF), lambda i, c: (layer, i, 0, 0)),
                  pl.BlockSpec((None, None, d, EXPERT_FF), lambda i, c: (layer, i, 0, 0)),
                  pl.BlockSpec((None, None, EXPERT_FF, d), lambda i, c: (layer, i, 0, 0))],
        out_specs=pl.BlockSpec((None, tile, d), lambda i, c: (i, c, 0)),
        out_shape=jax.ShapeDtypeStruct((e, cap, d), BF16),
        compiler_params=_cparams("parallel", "parallel"),
        name="moe_ffn",
    )(xs, w1, w3, w2)


def _combine_kernel(r0_ref, x_ref, slot_ref, lg_ref, lb_ref, ye_ref, o_ref,
                    win_ref, big_ref, acc_ref, sem, big_sem, *, cap):
    j = pl.program_id(0)
    n_blk = pl.num_programs(0)
    buf = j % 2
    experts = range(N_EXPERTS)

    def small_copy(start, b, e):
        return pltpu.make_async_copy(ye_ref.at[e, pl.ds(start, ROUTE_SMALL_WIN)],
                                     win_ref.at[b, pl.ds(e * ROUTE_SMALL_WIN, ROUTE_SMALL_WIN)],
                                     sem.at[b, e])

    starts, fits = _fits_small(r0_ref, experts, j, n_blk, cap, ROUTE_SMALL_WIN)
    nxt = jnp.minimum(j + 1, n_blk - 1)
    nxt_starts, nxt_fits = _fits_small(r0_ref, experts, nxt, n_blk, cap, ROUTE_SMALL_WIN)

    @pl.when((j == 0) & fits)
    def _():
        for e in experts:
            small_copy(starts[e], 0, e).start()

    @pl.when((j + 1 < n_blk) & nxt_fits)
    def _():
        for e in experts:
            small_copy(nxt_starts[e], 1 - buf, e).start()

    slot = slot_ref[...]

    def pick(e, start, width):
        lane = lax.broadcasted_iota(I32, (ROUTE_TILE, width), 1)
        return _one_hot(lane == slot[:, e:e + 1] - start)

    @pl.when(fits)
    def _():
        for e in experts:
            small_copy(starts[e], buf, e).wait()
        cols = []
        for e in experts:
            rel = slot[:, e:e + 1] - starts[e]
            cols.append(jnp.where((rel >= 0) & (rel < ROUTE_SMALL_WIN), rel + e * ROUTE_SMALL_WIN, -1))
        tiles = []
        for t in range(N_EXPERTS * ROUTE_SMALL_WIN // 128):
            lane = lax.broadcasted_iota(I32, (ROUTE_TILE, 128), 1) + t * 128
            hit = None
            for e in experts:
                if e * ROUTE_SMALL_WIN < (t + 1) * 128 and (e + 1) * ROUTE_SMALL_WIN > t * 128:
                    hit = lane == cols[e] if hit is None else hit | (lane == cols[e])
            tiles.append(_one_hot(hit))
        acc_ref[...] = jnp.dot(jnp.concatenate(tiles, axis=1), win_ref[buf], preferred_element_type=F32)

    @pl.when(jnp.logical_not(fits))
    def _():
        acc = jnp.zeros((ROUTE_TILE, D_MODEL), F32)
        for e in experts:
            start = _window_start(r0_ref[e, j], cap, ROUTE_WIN)
            copy = pltpu.make_async_copy(ye_ref.at[e, pl.ds(start, ROUTE_WIN)], big_ref, big_sem)
            copy.start()
            copy.wait()
            acc = acc + jnp.dot(pick(e, start, ROUTE_WIN), big_ref[...], preferred_element_type=F32)
        acc_ref[...] = acc

    o_ref[...] = _layer_norm(DN_ALPHA * x_ref[...] + acc_ref[...], lg_ref[...], lb_ref[...])


def _combine(x, slot, ye, r0, ln_g, ln_b, cap):
    n, d = x.shape
    n_blk = n // ROUTE_TILE
    grid_spec = pltpu.PrefetchScalarGridSpec(
        num_scalar_prefetch=1,
        grid=(n_blk,),
        in_specs=[pl.BlockSpec((ROUTE_TILE, d), lambda j, r0: (j, 0)),
                  pl.BlockSpec((ROUTE_TILE, N_EXPERTS), lambda j, r0: (j, 0)),
                  pl.BlockSpec((1, d), lambda j, r0: (0, 0)),
                  pl.BlockSpec((1, d), lambda j, r0: (0, 0)),
                  pl.BlockSpec(memory_space=pl.ANY)],
        out_specs=pl.BlockSpec((ROUTE_TILE, d), lambda j, r0: (j, 0)),
        scratch_shapes=[pltpu.VMEM((2, N_EXPERTS * ROUTE_SMALL_WIN, d), BF16),
                        pltpu.VMEM((ROUTE_WIN, d), BF16),
                        pltpu.VMEM((ROUTE_TILE, d), F32),
                        pltpu.SemaphoreType.DMA((2, N_EXPERTS)),
                        pltpu.SemaphoreType.DMA(())],
    )
    return pl.pallas_call(
        functools.partial(_combine_kernel, cap=cap),
        grid_spec=grid_spec,
        out_shape=jax.ShapeDtypeStruct((n, d), F32),
        compiler_params=_cparams("arbitrary"),
        name="moe_combine",
    )(r0, x, slot, ln_g.reshape(1, d), ln_b.reshape(1, d), ye)


def _moe_layer(x, xb, aff_t, w1, w3, w2, layer, ln_g, ln_b):
    bsz, seq, d = x.shape
    n = bsz * seq
    cap = EC_CAPACITY_FACTOR * n // N_EXPERTS
    slot_t, r0 = _route(aff_t, cap)
    xs = _dispatch(xb.reshape(n, d), slot_t, aff_t, r0, cap)
    ye = _expert_ffn(xs, w1, w3, w2, layer)
    out = _combine(x.reshape(n, d), slot_t.T, ye, r0, ln_g, ln_b, cap)
    return out.reshape(bsz, seq, d)


def _trunk(x, mem, p):
    bsz = x.shape[0]
    m = mem.shape[1]
    for i in range(DEPTH):
        if i % 2 == 0:
            x = _s5_layer(x, p["a_w_in"][i // 2], p["s5_ops"][i // 2], p["a_w_glu"][i // 2],
                          p["ln_g"][i, 0], p["ln_b"][i, 0])
        else:
            x = _na_layer(x, p["b_w_qkv"][i // 2], p["na_bias"][i // 2], p["b_w_o"][i // 2],
                          p["ln_g"][i, 0], p["ln_b"][i, 0])
        kv = _matmul(mem.reshape(bsz * m, D_MODEL), p["m_w_kv"][i], BF16).reshape(bsz, m, 2 * D_MODEL)
        x, xb, aff_t = _xattn_router(x, kv[:, :, :D_MODEL], kv[:, :, D_MODEL:], p["m_w_q"][i], p["m_w_o"][i],
                                     p["ln_g"][i, 1], p["ln_b"][i, 1], p["e_w_router_t"][i])
        x = _moe_layer(x, xb, aff_t, p["e_w1"], p["e_w3"], p["e_w2"], i,
                       p["ln_g"][i, 2], p["ln_b"][i, 2])
    return x


def kernel(x_prompt, x_sample, mem_prompt, mem_sample, a_w_in, a_lam_re, a_lam_im, a_log_dt, a_b_re, a_b_im, a_c_re, a_c_im, a_d, a_w_glu, b_w_qkv, b_rpb, b_w_o, m_w_q, m_w_kv, m_w_o, e_w_router, e_w1, e_w3, e_w2, ln_g, ln_b):
    bf = lambda w: w.astype(BF16)
    p = {
        "a_w_in": bf(a_w_in), "a_w_glu": bf(a_w_glu),
        "s5_ops": [_s5_operators(a_lam_re[j], a_lam_im[j], a_log_dt[j], a_b_re[j], a_b_im[j],
                                 a_c_re[j], a_c_im[j], a_d[j]) for j in range(a_w_in.shape[0])],
        "b_w_qkv": bf(b_w_qkv), "b_w_o": bf(b_w_o),
        "na_bias": [_na_bias_table(b_rpb[j]) for j in range(b_rpb.shape[0])],
        "m_w_q": bf(m_w_q), "m_w_kv": bf(m_w_kv), "m_w_o": bf(m_w_o),
        "e_w_router_t": bf(jnp.swapaxes(e_w_router, 1, 2)),
        "e_w1": e_w1, "e_w3": e_w3, "e_w2": e_w2,
        "ln_g": ln_g.astype(F32), "ln_b": ln_b.astype(F32),
    }
    return (_trunk(x_prompt, mem_prompt, p), _trunk(x_sample, mem_sample, p))
```

```python
import functools

import jax
import jax.numpy as jnp
from jax import lax
from jax.experimental import pallas as pl
from jax.experimental.pallas import tpu as pltpu

F32 = jnp.float32
BF16 = jnp.bfloat16
I32 = jnp.int32

D_MODEL = 1024
DEPTH = 2
GRID_W = 64
S5_GROUP = 16
S5_GROUPS = D_MODEL // S5_GROUP
S5_STATE = 64
S5_CHUNK = 16
S5_GROUPS_PER_STEP = 4
NA_HEADS = 16
NA_HEAD_DIM = D_MODEL // NA_HEADS
NA_WIN_ROWS = 8
NA_WIN_COLS = 16
NA_QCOLS = 16
NA_KCOLS = 32
NA_ROW_BLOCK = 8
NA_KEY_ROWS = 16
NA_MASKED = -1e30
MEM_HEADS = 4
MEM_HEAD_DIM = D_MODEL // MEM_HEADS
N_EXPERTS = 16
EXPERT_FF = 2048
EC_CAPACITY_FACTOR = 2
ROUTE_TILE = 256
ROUTE_ALIGN = 16
ROUTE_WIN = ROUTE_TILE + ROUTE_ALIGN
ROUTE_SMALL_WIN = 80
DISPATCH_SMALL_WIN = 64
DISPATCH_VMEM_BYTES = 40 * 1024 * 1024
DISPATCH_TOKENS = 2048
GATE_LANES = 128
FFN_TILE = 512
FF_CHUNK = 512
DN_ALPHA = (2.0 * DEPTH) ** 0.25
LN_EPS = 1e-5

ROW_TILE = 512
XATTN_TILE = 1024
VMEM_LIMIT = 56 * 1024 * 1024


def _cparams(*sem):
    return pltpu.CompilerParams(dimension_semantics=sem, vmem_limit_bytes=VMEM_LIMIT)


def _layer_norm(v, g, b):
    mu = jnp.mean(v, axis=-1, keepdims=True)
    c = v - mu
    var = jnp.mean(c * c, axis=-1, keepdims=True)
    return c * lax.rsqrt(var + LN_EPS) * g + b


def _matmul_kernel(x_ref, w_ref, o_ref):
    o_ref[...] = jnp.dot(x_ref[...].astype(BF16), w_ref[...],
                         preferred_element_type=F32).astype(o_ref.dtype)


def _matmul(x, w, out_dtype, tile=ROW_TILE):
    n, k = x.shape
    m = w.shape[1]
    tile = min(tile, n)
    return pl.pallas_call(
        _matmul_kernel,
        grid=(n // tile,),
        in_specs=[pl.BlockSpec((tile, k), lambda i: (i, 0)),
                  pl.BlockSpec((k, m), lambda i: (0, 0))],
        out_specs=pl.BlockSpec((tile, m), lambda i: (i, 0)),
        out_shape=jax.ShapeDtypeStruct((n, m), out_dtype),
        compiler_params=_cparams("parallel"),
        name="matmul",
    )(x, w)


def _s5_ops_kernel(pw_re_ref, pw_im_ref, pws_re_ref, pws_im_ref, bt_re_ref, bt_im_ref,
                   btb_re_ref, btb_im_ref, ct_re_ref, ct_im_ref, t_ref, b_ref, c_ref):
    n = S5_CHUNK
    w = n * S5_GROUP
    exact = lax.Precision.HIGHEST
    lane = lax.broadcasted_iota(I32, (S5_STATE, w), 1)
    k_blk = lane // S5_GROUP
    c_idx = lane % S5_GROUP
    t_lane = lax.broadcasted_iota(I32, (S5_GROUP, w), 1)

    def spread_k(pw, k_of_lane):
        out = jnp.zeros((S5_STATE, w), F32)
        for k in range(n + 1):
            out = jnp.where(k_of_lane == k, pw[:, k:k + 1], out)
        return out

    def spread_c(ct):
        out = jnp.zeros((S5_STATE, w), F32)
        for c in range(S5_GROUP):
            out = jnp.where(c_idx == c, ct[:, c:c + 1], out)
        return out

    for gi in range(t_ref.shape[0]):
        kern = []
        for d in range(2):
            pw_re, pw_im = pw_re_ref[d, gi], pw_im_ref[d, gi]
            c_re, c_im = spread_c(ct_re_ref[d, gi]), spread_c(ct_im_ref[d, gi])
            lag = k_blk if d == 0 else (n - 1) - k_blk
            out = k_blk + 1 if d == 0 else n - k_blk
            a_re, a_im = spread_k(pw_re, lag), spread_k(pw_im, lag)
            cp_re, cp_im = c_re * a_re - c_im * a_im, c_re * a_im + c_im * a_re
            kern.append(jnp.dot(bt_re_ref[d, gi], cp_re, precision=exact, preferred_element_type=F32)
                        - jnp.dot(bt_im_ref[d, gi], cp_im, precision=exact, preferred_element_type=F32))
            a_re, a_im = spread_k(pw_re, out), spread_k(pw_im, out)
            c_ref[gi, d * S5_STATE:(d + 1) * S5_STATE, :] = (c_re * a_re - c_im * a_im).astype(BF16)
            c_ref[gi, (2 + d) * S5_STATE:(3 + d) * S5_STATE, :] = (-(c_re * a_im + c_im * a_re)).astype(BF16)
        kern_f, kern_b = kern
        bt_re, bt_im = btb_re_ref[gi], btb_im_ref[gi]
        for s in range(n):
            rows = slice(s * S5_GROUP, (s + 1) * S5_GROUP)
            fwd = pltpu.roll(kern_f, s * S5_GROUP, 1) if s else kern_f
            fwd = jnp.where(t_lane >= s * S5_GROUP, fwd, 0.0)
            shift = ((s + 1) * S5_GROUP) % w
            bwd = pltpu.roll(kern_b, shift, 1) if shift else kern_b
            bwd = jnp.where(t_lane < (s + 1) * S5_GROUP, bwd, 0.0)
            t_ref[gi, rows, :] = (fwd + bwd).astype(BF16)
            p_re, p_im = pws_re_ref[gi, s:s + 1, :], pws_im_ref[gi, s:s + 1, :]
            b_ref[gi, rows, :2 * S5_STATE] = (p_re * bt_re - p_im * bt_im).astype(BF16)
            b_ref[gi, rows, 2 * S5_STATE:] = (p_re * bt_im + p_im * bt_re).astype(BF16)


def _s5_operators(lam_re, lam_im, log_dt, b_re, b_im, c_re, c_im, d_skip):
    n = S5_CHUNK
    dt = jnp.exp(log_dt.astype(F32))[..., None]
    lr = lam_re.astype(F32) * dt
    li = lam_im.astype(F32) * dt
    k = jnp.arange(n + 1, dtype=F32)
    mag = jnp.exp(lr[..., None] * k)
    pw_re = mag * jnp.cos(li[..., None] * k)
    pw_im = mag * jnp.sin(li[..., None] * k)
    x = pw_re[..., 1] - 1.0
    y = pw_im[..., 1]
    a = lam_re.astype(F32)
    b = lam_im.astype(F32)
    den = a * a + b * b
    q_re = ((x * a + y * b) / den)[..., None]
    q_im = ((y * a - x * b) / den)[..., None]
    bb_re = q_re * b_re.astype(F32) - q_im * b_im.astype(F32)
    bb_im = q_re * b_im.astype(F32) + q_im * b_re.astype(F32)
    bt_re = jnp.swapaxes(bb_re, 2, 3)
    bt_im = jnp.swapaxes(bb_im, 2, 3)
    ct_re = jnp.swapaxes(c_re.astype(F32), 2, 3)
    ct_im = jnp.swapaxes(c_im.astype(F32), 2, 3)
    both = lambda fwd, bwd: jnp.concatenate([fwd, bwd], axis=-1)
    pwt_re = jnp.swapaxes(pw_re[..., :n], 2, 3)
    pwt_im = jnp.swapaxes(pw_im[..., :n], 2, 3)
    pws_re = both(jnp.flip(pwt_re[0], axis=1), pwt_re[1])
    pws_im = both(jnp.flip(pwt_im[0], axis=1), pwt_im[1])
    gb = S5_GROUPS_PER_STEP
    per_dir = lambda shape: pl.BlockSpec((2, gb) + shape, lambda i: (0, i, 0, 0))
    per_grp = lambda shape: pl.BlockSpec((gb,) + shape, lambda i: (i, 0, 0))
    w = n * S5_GROUP
    t_op, b_op, c_op = pl.pallas_call(
        _s5_ops_kernel,
        grid=(S5_GROUPS // gb,),
        in_specs=[per_dir((S5_STATE, n + 1)), per_dir((S5_STATE, n + 1)),
                  per_grp((n, 2 * S5_STATE)), per_grp((n, 2 * S5_STATE)),
                  per_dir((S5_GROUP, S5_STATE)), per_dir((S5_GROUP, S5_STATE)),
                  per_grp((S5_GROUP, 2 * S5_STATE)), per_grp((S5_GROUP, 2 * S5_STATE)),
                  per_dir((S5_STATE, S5_GROUP)), per_dir((S5_STATE, S5_GROUP))],
        out_specs=[per_grp((w, w))] * 3,
        out_shape=[jax.ShapeDtypeStruct((S5_GROUPS, w, w), BF16)] * 3,
        compiler_params=_cparams("parallel"),
        name="s5_operators",
    )(pw_re, pw_im, pws_re, pws_im, bt_re, bt_im, both(bt_re[0], bt_re[1]), both(bt_im[0], bt_im[1]),
      ct_re, ct_im)
    a_re = jnp.concatenate([pw_re[0][..., n], pw_re[1][..., n]], axis=-1)
    a_im = jnp.concatenate([pw_im[0][..., n], pw_im[1][..., n]], axis=-1)
    a_op = jnp.stack([a_re, a_im], axis=1)
    a_op = jnp.broadcast_to(a_op[:, :, None, :], (S5_GROUPS, 2, 8, 128))
    d_op = jnp.tile(d_skip.astype(F32).reshape(S5_GROUPS, 1, 16), (1, 1, n))
    return t_op, b_op, c_op, a_op, d_op


def _s5_kernel(u_ref, t_ref, b_ref, c_ref, a_ref, d_ref, o_ref, s_scr, hc_scr, *, n_tiles):
    gb = u_ref.shape[0]
    for gi in range(gb):
        s_scr[gi] = jnp.dot(u_ref[gi].astype(BF16), b_ref[gi], preferred_element_type=F32)

    row = lax.broadcasted_iota(I32, (8, 128), 0)
    lane = lax.broadcasted_iota(I32, (8, 128), 1)
    low_rows = row < 4
    fwd_lanes = lane < S5_STATE
    a_re = [a_ref[gi, 0] for gi in range(gb)]
    a_im = [a_ref[gi, 1] for gi in range(gb)]

    def cmul_add(gi, h_re, h_im, v_re, v_im):
        return (a_re[gi] * h_re - a_im[gi] * h_im + v_re,
                a_re[gi] * h_im + a_im[gi] * h_re + v_im)

    def half_steps(gi, k, c_re, c_im):
        rows = pl.ds(pl.multiple_of(k * 8, 8), 8)
        v_re = s_scr[gi, rows, 0:128]
        v_im = s_scr[gi, rows, 128:256]
        h1_re, h1_im = cmul_add(gi, c_re, c_im, v_re, v_im)
        h1s_re = pltpu.roll(h1_re, 4, 0)
        h1s_im = pltpu.roll(h1_im, 4, 0)
        h2_re, h2_im = cmul_add(gi, h1s_re, h1s_im, v_re, v_im)
        return rows, h1s_re, h1s_im, pltpu.roll(h2_re, 4, 0), pltpu.roll(h2_im, 4, 0)

    def fwd_body(k, carry):
        out = []
        for gi in range(gb):
            c_re, c_im = carry[2 * gi], carry[2 * gi + 1]
            rows, h1s_re, h1s_im, n_re, n_im = half_steps(gi, k, c_re, c_im)
            hc_scr[gi, rows, 0:128] = jnp.where(low_rows, c_re, h1s_re)
            hc_scr[gi, rows, 128:256] = jnp.where(low_rows, c_im, h1s_im)
            out += [n_re, n_im]
        return tuple(out)

    def bwd_body(i, carry):
        k = n_tiles - 1 - i
        out = []
        for gi in range(gb):
            c_re, c_im = carry[2 * gi], carry[2 * gi + 1]
            rows, h1s_re, h1s_im, n_re, n_im = half_steps(gi, k, c_re, c_im)
            new_re = jnp.where(low_rows, h1s_re, c_re)
            new_im = jnp.where(low_rows, h1s_im, c_im)
            hc_scr[gi, rows, 0:128] = jnp.where(fwd_lanes, hc_scr[gi, rows, 0:128], new_re)
            hc_scr[gi, rows, 128:256] = jnp.where(fwd_lanes, hc_scr[gi, rows, 128:256], new_im)
            out += [n_re, n_im]
        return tuple(out)

    zeros = tuple(jnp.zeros((8, 128), F32) for _ in range(2 * gb))
    lax.fori_loop(0, n_tiles, fwd_body, zeros)
    lax.fori_loop(0, n_tiles, bwd_body, zeros)

    for gi in range(gb):
        u = u_ref[gi]
        y = (jnp.dot(u.astype(BF16), t_ref[gi], preferred_element_type=F32)
             + jnp.dot(hc_scr[gi].astype(BF16), c_ref[gi], preferred_element_type=F32)
             + d_ref[gi] * u)
        o_ref[gi] = jax.nn.gelu(y).astype(o_ref.dtype)


def _s5_scan(u_rows, ops):
    t_op, b_op, c_op, a_op, d_op = ops
    g, m, w = u_rows.shape
    gb = S5_GROUPS_PER_STEP
    spec3 = lambda shape: pl.BlockSpec((gb,) + shape, lambda i: (i,) + (0,) * len(shape))
    return pl.pallas_call(
        functools.partial(_s5_kernel, n_tiles=m // 8),
        grid=(g // gb,),
        in_specs=[spec3((m, w)), spec3((w, w)), spec3((w, w)), spec3((w, w)),
                  spec3((2, 8, 128)), spec3((1, w))],
        out_specs=spec3((m, w)),
        out_shape=jax.ShapeDtypeStruct((g, m, w), F32),
        scratch_shapes=[pltpu.VMEM((gb, m, w), F32), pltpu.VMEM((gb, m, w), F32)],
        compiler_params=_cparams("parallel"),
        name="s5_scan",
    )(u_rows, t_op, b_op, c_op, a_op, d_op)


def _block_transpose(x):
    rows, lanes = x.ndim - 2, x.ndim - 1
    row = lax.broadcasted_iota(I32, x.shape, rows)
    blk = lax.broadcasted_iota(I32, x.shape, lanes) // S5_GROUP
    for d in (1, 2, 4):
        up = pltpu.roll(pltpu.roll(x, 8 - d, rows), S5_GROUP * d, lanes)
        down = pltpu.roll(pltpu.roll(x, d, rows), 128 - S5_GROUP * d, lanes)
        col_bit = (blk & d) != 0
        x = jnp.where((row & d) == (blk & d), x, jnp.where(col_bit, up, down))
    return x


def _s5_in_kernel(x_ref, w_ref, o_ref):
    bsz, steps, d = x_ref.shape
    n_chunk = steps // S5_CHUNK
    u = jnp.dot(x_ref[...].reshape(bsz * steps, d).astype(BF16), w_ref[...], preferred_element_type=F32)
    u = u.reshape(bsz, n_chunk, S5_CHUNK, d)
    for s in range(S5_CHUNK // 8):
        for c in range(d // 128):
            w = _block_transpose(u[:, :, s * 8:(s + 1) * 8, c * 128:(c + 1) * 128])
            for jj in range(n_chunk):
                for b in range(bsz):
                    o_ref[c * 8:(c + 1) * 8, jj * bsz + b, s * 128:(s + 1) * 128] = w[b, jj]


def _s5_glu_ln_kernel(g_ref, x_ref, w_ref, lg_ref, lb_ref, o_ref, tok_scr):
    bsz, steps, d = x_ref.shape
    n_chunk = steps // S5_CHUNK
    for s in range(S5_CHUNK // 8):
        for c in range(d // 128):
            w = jnp.stack([jnp.stack([g_ref[c * 8:(c + 1) * 8, jj * bsz + b, s * 128:(s + 1) * 128]
                                      for jj in range(n_chunk)]) for b in range(bsz)])
            tok_scr[:, :, s * 8:(s + 1) * 8, c * 128:(c + 1) * 128] = _block_transpose(w)
    g = tok_scr[...].reshape(bsz * steps, d).astype(BF16)
    z = jnp.dot(g, w_ref[...], preferred_element_type=F32)
    h = z[:, :d] * jax.nn.sigmoid(z[:, d:])
    x = x_ref[...].reshape(bsz * steps, d)
    o_ref[...] = _layer_norm(DN_ALPHA * x + h, lg_ref[...], lb_ref[...]).reshape(bsz, steps, d)


def _s5_layer(x, w_in, ops, w_glu, ln_g, ln_b):
    bsz, seq, d = x.shape
    assert bsz == 4, "the chunk-state scan packs two chunk steps x four batches into one 8-row tile"
    j = seq // S5_CHUNK
    steps = ROW_TILE // bsz
    n_chunk = steps // S5_CHUNK
    width = S5_CHUNK * S5_GROUP
    tok_spec = pl.BlockSpec((bsz, steps, d), lambda i: (0, i, 0))
    row_spec = pl.BlockSpec((S5_GROUPS, n_chunk * bsz, width), lambda i: (0, i, 0))
    const = lambda shape: pl.BlockSpec(shape, lambda i: (0, 0))
    u_rows = pl.pallas_call(
        _s5_in_kernel,
        grid=(seq // steps,),
        in_specs=[tok_spec, const((d, d))],
        out_specs=row_spec,
        out_shape=jax.ShapeDtypeStruct((S5_GROUPS, j * bsz, width), F32),
        compiler_params=_cparams("parallel"),
        name="s5_in_proj",
    )(x, w_in)
    g_rows = _s5_scan(u_rows, ops)
    return pl.pallas_call(
        _s5_glu_ln_kernel,
        grid=(seq // steps,),
        in_specs=[row_spec, tok_spec, const((d, 2 * d)), const((1, d)), const((1, d))],
        out_specs=tok_spec,
        out_shape=jax.ShapeDtypeStruct((bsz, seq, d), F32),
        scratch_shapes=[pltpu.VMEM((bsz, n_chunk, S5_CHUNK, d), F32)],
        compiler_params=_cparams("parallel"),
        name="s5_glu_ln",
    )(g_rows, x, w_glu, ln_g.reshape(1, d), ln_b.reshape(1, d))


def _na_key_col_starts():
    n_cb = GRID_W // NA_QCOLS
    return [min(max(n * NA_QCOLS - NA_WIN_COLS // 2, 0), GRID_W - NA_KCOLS) for n in range(n_cb)]


def _na_bias_table(rpb):
    exact = lax.Precision.HIGHEST
    n_cb = GRID_W // NA_QCOLS
    starts = jnp.asarray(_na_key_col_starts(), I32)[:, None, None]
    half = NA_WIN_ROWS // 2
    rl = jnp.arange(NA_ROW_BLOCK)[:, None]
    kl = jnp.arange(NA_KEY_ROWS)[None, :]
    dr = jnp.clip(kl - rl + (NA_WIN_ROWS - 1 - half), 0, 2 * NA_WIN_ROWS - 2)
    qc = jnp.arange(n_cb)[:, None, None] * NA_QCOLS + jnp.arange(NA_QCOLS)[None, :, None]
    kc = starts + jnp.arange(NA_KCOLS)[None, None, :]
    dc = jnp.clip(kc - qc + NA_WIN_COLS - 1, 0, 2 * NA_WIN_COLS - 2)
    pick_dc = (dc[..., None] == jnp.arange(2 * NA_WIN_COLS - 1)).astype(F32)
    pick_dr = (dr[..., None] == jnp.arange(2 * NA_WIN_ROWS - 1)).astype(F32)
    by_col = jnp.einsum('hrc,nqkc->hrnqk', rpb.astype(F32), pick_dc, precision=exact)
    bias = jnp.einsum('hrnqk,alr->nhaqlk', by_col, pick_dr, precision=exact)
    nq = NA_ROW_BLOCK * NA_QCOLS
    nk = NA_KEY_ROWS * NA_KCOLS
    bias = bias.reshape(n_cb, 1, NA_HEADS // 2, 2 * nq, nk)
    row = jnp.arange(2 * nq)[:, None]
    col = jnp.arange(nk)[None, :]
    f_rl, f_qcl = (row % nq) // NA_QCOLS, row % NA_QCOLS
    f_kl, f_kcl = col // NA_KCOLS, col % NA_KCOLS
    f_rs = jnp.stack([jnp.maximum(f_rl - half, 0), f_rl - half, jnp.minimum(f_rl - half, 0)])
    f_row_ok = (f_kl - half >= f_rs) & (f_kl - half < f_rs + NA_WIN_ROWS)
    blk = jnp.arange(n_cb)[:, None, None]
    f_qc = blk * NA_QCOLS + f_qcl
    f_kc = starts + f_kcl
    f_ws = jnp.clip(f_qc - NA_WIN_COLS // 2, 0, GRID_W - NA_WIN_COLS)
    f_col_ok = (f_kc >= f_ws) & (f_kc < f_ws + NA_WIN_COLS)
    ok = f_row_ok[None, :, None] & f_col_ok[:, None, None]
    return jnp.where(ok, bias, NA_MASKED)


def _qkv_kernel(x_ref, w_ref, q_ref, k_ref, v_ref):
    qkv = jnp.dot(x_ref[...].astype(BF16), w_ref[...], preferred_element_type=F32)
    q_ref[...] = (qkv[:, :D_MODEL] * (NA_HEAD_DIM ** -0.5)).astype(BF16)
    k = qkv[:, D_MODEL:2 * D_MODEL]
    v = qkv[:, 2 * D_MODEL:]
    for r in range(NA_ROW_BLOCK):
        for n, start in enumerate(_na_key_col_starts()):
            lo = r * GRID_W + start
            k_ref[r, n] = k[lo:lo + NA_KCOLS].astype(BF16)
            v_ref[r, n] = v[lo:lo + NA_KCOLS].astype(BF16)


def _na_kernel(q_ref, kp_ref, kc_ref, kn_ref, vp_ref, vc_ref, vn_ref, bias_ref, o_ref, k_scr, v_scr):
    nq = NA_ROW_BLOCK * NA_QCOLS
    quarter = 4 * NA_KCOLS
    k_scr[0:quarter] = kp_ref[...].reshape(quarter, D_MODEL)
    k_scr[quarter:3 * quarter] = kc_ref[...].reshape(2 * quarter, D_MODEL)
    k_scr[3 * quarter:] = kn_ref[...].reshape(quarter, D_MODEL)
    v_scr[0:quarter] = vp_ref[...].reshape(quarter, D_MODEL)
    v_scr[quarter:3 * quarter] = vc_ref[...].reshape(2 * quarter, D_MODEL)
    v_scr[3 * quarter:] = vn_ref[...].reshape(quarter, D_MODEL)
    q = q_ref[...].reshape(nq, D_MODEL)
    first_head = lax.broadcasted_iota(I32, (nq, 128), 1) < NA_HEAD_DIM
    zero = jnp.zeros((nq, 128), BF16)
    for hp in range(NA_HEADS // 2):
        lanes = slice(hp * 128, (hp + 1) * 128)
        q2 = q[:, lanes]
        qs = jnp.concatenate([jnp.where(first_head, q2, zero), jnp.where(first_head, zero, q2)], axis=0)
        s = lax.dot_general(qs, k_scr[:, lanes], (((1,), (1,)), ((), ())),
                            preferred_element_type=F32) + bias_ref[hp]
        m = jnp.max(s, axis=-1, keepdims=True)
        p = jnp.exp(s - m)
        l = jnp.sum(p, axis=-1, keepdims=True)
        o = jnp.dot(p.astype(BF16), v_scr[:, lanes], preferred_element_type=F32) / l
        o_ref[:, :, lanes] = jnp.where(first_head, o[:nq], o[nq:]).astype(BF16).reshape(
            NA_ROW_BLOCK, NA_QCOLS, 128)


def _na_attention(x, w_qkv, bias_table):
    bsz, seq, d = x.shape
    rows = seq // GRID_W
    n_cb = GRID_W // NA_QCOLS
    n_rb = rows // NA_ROW_BLOCK
    assert n_rb >= 2
    tile = NA_ROW_BLOCK * GRID_W
    q, kx, vx = pl.pallas_call(
        _qkv_kernel,
        grid=(bsz, n_rb),
        in_specs=[pl.BlockSpec((None, tile, d), lambda b, i: (b, i, 0)),
                  pl.BlockSpec((d, 3 * d), lambda b, i: (0, 0))],
        out_specs=[pl.BlockSpec((None, tile, d), lambda b, i: (b, i, 0)),
                   pl.BlockSpec((None, NA_ROW_BLOCK, n_cb, NA_KCOLS, d), lambda b, i: (b, i, 0, 0, 0)),
                   pl.BlockSpec((None, NA_ROW_BLOCK, n_cb, NA_KCOLS, d), lambda b, i: (b, i, 0, 0, 0))],
        out_shape=[jax.ShapeDtypeStruct((bsz, seq, d), BF16),
                   jax.ShapeDtypeStruct((bsz, rows, n_cb, NA_KCOLS, d), BF16),
                   jax.ShapeDtypeStruct((bsz, rows, n_cb, NA_KCOLS, d), BF16)],
        compiler_params=_cparams("parallel", "parallel"),
        name="na_qkv",
    )(x, w_qkv)

    q5 = q.reshape(bsz, rows, n_cb, NA_QCOLS, d)
    n_half = rows // 4
    half_view = lambda a: a.reshape(bsz, n_half, 4, n_cb, NA_KCOLS, d)
    full_view = lambda a: a.reshape(bsz, n_rb, NA_ROW_BLOCK, n_cb, NA_KCOLS, d)
    prev_spec = pl.BlockSpec((None, None, 4, None, NA_KCOLS, d),
                             lambda n, b, i: (b, jnp.maximum(2 * i - 1, 0), 0, n, 0, 0))
    cur_spec = pl.BlockSpec((None, None, NA_ROW_BLOCK, None, NA_KCOLS, d),
                            lambda n, b, i: (b, i, 0, n, 0, 0))
    next_spec = pl.BlockSpec((None, None, 4, None, NA_KCOLS, d),
                             lambda n, b, i: (b, jnp.minimum(2 * i + 2, n_half - 1), 0, n, 0, 0))
    kind = lambda i: jnp.where(i == 0, 0, jnp.where(i == n_rb - 1, 2, 1))
    nk = NA_KEY_ROWS * NA_KCOLS
    o5 = pl.pallas_call(
        _na_kernel,
        grid=(n_cb, bsz, n_rb),
        in_specs=[pl.BlockSpec((None, NA_ROW_BLOCK, None, NA_QCOLS, d), lambda n, b, i: (b, i, n, 0, 0)),
                  prev_spec, cur_spec, next_spec, prev_spec, cur_spec, next_spec,
                  pl.BlockSpec((None, None) + bias_table.shape[2:],
                               lambda n, b, i: (n, kind(i), 0, 0, 0))],
        out_specs=pl.BlockSpec((None, NA_ROW_BLOCK, None, NA_QCOLS, d), lambda n, b, i: (b, i, n, 0, 0)),
        out_shape=jax.ShapeDtypeStruct((bsz, rows, n_cb, NA_QCOLS, d), BF16),
        scratch_shapes=[pltpu.VMEM((nk, d), BF16), pltpu.VMEM((nk, d), BF16)],
        compiler_params=_cparams("parallel", "parallel", "parallel"),
        name="na_attn",
    )(q5, half_view(kx), full_view(kx), half_view(kx), half_view(vx), full_view(vx), half_view(vx),
      bias_table)
    return o5.reshape(bsz, seq, d)


def _xattn_kernel(x_ref, *rest):
    _xattn_body(x_ref[...], *rest)


def _proj_xattn_kernel(h_ref, wp_ref, lg0_ref, lb0_ref, x_ref, *rest):
    h = jnp.dot(h_ref[...], wp_ref[...], preferred_element_type=F32)
    _xattn_body(_layer_norm(DN_ALPHA * x_ref[...] + h, lg0_ref[...], lb0_ref[...]), *rest)


def _xattn_body(x, k_ref, v_ref, wq_ref, wo_ref, lg_ref, lb_ref, wr_ref, o_ref, ob_ref, aff_ref):
    q = jnp.dot(x.astype(BF16), wq_ref[...], preferred_element_type=F32).astype(BF16)
    heads = []
    for h in range(MEM_HEADS):
        lanes = slice(h * MEM_HEAD_DIM, (h + 1) * MEM_HEAD_DIM)
        s = lax.dot_general(q[:, lanes], k_ref[:, lanes], (((1,), (1,)), ((), ())),
                            preferred_element_type=F32) * (MEM_HEAD_DIM ** -0.5)
        m = jnp.max(s, axis=-1, keepdims=True)
        p = jnp.exp(s - m)
        p = p / jnp.sum(p, axis=-1, keepdims=True)
        heads.append(jnp.dot(p.astype(BF16), v_ref[:, lanes], preferred_element_type=F32).astype(BF16))
    o = jnp.concatenate(heads, axis=-1)
    r = jnp.dot(o, wo_ref[...], preferred_element_type=F32)
    y = _layer_norm(DN_ALPHA * x + r, lg_ref[...], lb_ref[...])
    o_ref[...] = y
    yb = y.astype(BF16)
    ob_ref[...] = yb
    logits = lax.dot_general(wr_ref[...], yb, (((1,), (1,)), ((), ())), preferred_element_type=F32)
    e = jnp.exp(logits - jnp.max(logits, axis=0, keepdims=True))
    aff_ref[...] = e / jnp.sum(e, axis=0, keepdims=True)


def _xattn_router(x, mem_k, mem_v, w_q, w_o, ln_g, ln_b, w_router_t, pre=None):
    bsz, seq, d = x.shape
    m = mem_k.shape[1]
    tile = min(XATTN_TILE, seq)
    n_t = seq // tile
    const = lambda shape: pl.BlockSpec(shape, lambda b, i: (0, 0))
    tok = pl.BlockSpec((None, tile, d), lambda b, i: (b, i, 0))
    pre_specs = [] if pre is None else [tok, const((d, d)), const((1, d)), const((1, d))]
    pre_args = [] if pre is None else [pre[0], pre[1], pre[2].reshape(1, d), pre[3].reshape(1, d)]
    return pl.pallas_call(
        _xattn_kernel if pre is None else _proj_xattn_kernel,
        grid=(bsz, n_t),
        in_specs=pre_specs + [
                  pl.BlockSpec((None, tile, d), lambda b, i: (b, i, 0)),
                  pl.BlockSpec((None, m, d), lambda b, i: (b, 0, 0)),
                  pl.BlockSpec((None, m, d), lambda b, i: (b, 0, 0)),
                  const((d, d)), const((d, d)), const((1, d)), const((1, d)), const((N_EXPERTS, d))],
        out_specs=[pl.BlockSpec((None, tile, d), lambda b, i: (b, i, 0)),
                   pl.BlockSpec((None, tile, d), lambda b, i: (b, i, 0)),
                   pl.BlockSpec((N_EXPERTS, tile), lambda b, i: (0, b * n_t + i))],
        out_shape=[jax.ShapeDtypeStruct((bsz, seq, d), F32),
                   jax.ShapeDtypeStruct((bsz, seq, d), BF16),
                   jax.ShapeDtypeStruct((N_EXPERTS, bsz * seq), F32)],
        compiler_params=_cparams("parallel", "parallel"),
        name="xattn_router",
    )(*pre_args, x, mem_k, mem_v, w_q, w_o, ln_g.reshape(1, d), ln_b.reshape(1, d), w_router_t)


def _route_kernel(aff_ref, slot_ref, r0_ref, *, cap):
    n = aff_ref.shape[1]
    n_blk = n // ROUTE_TILE
    aff = aff_ref[...]

    def count(mask):
        return jnp.sum(jnp.where(mask, 1.0, 0.0), axis=1, keepdims=True)

    def search(i, bits):
        cand = bits | lax.shift_left(jnp.int32(1), 30 - i)
        ge = aff >= lax.bitcast_convert_type(cand, F32)
        return jnp.where(count(ge) >= cap, cand, bits)

    tau = lax.bitcast_convert_type(
        lax.fori_loop(0, 31, search, jnp.zeros((N_EXPERTS, 1), I32)), F32)
    need = cap - count(aff > tau)
    upper = (lax.broadcasted_iota(I32, (ROUTE_TILE, ROUTE_TILE), 0)
             < lax.broadcasted_iota(I32, (ROUTE_TILE, ROUTE_TILE), 1)).astype(BF16)
    blk_lane = lax.broadcasted_iota(I32, (N_EXPERTS, n_blk), 1)

    r0_ref[...] = jnp.zeros_like(r0_ref)

    def block(kb, carry):
        c_eq, c_sel = carry
        cols = pl.ds(pl.multiple_of(kb * ROUTE_TILE, ROUTE_TILE), ROUTE_TILE)
        b = aff_ref[:, cols]
        eq = b == tau
        eq_f = jnp.where(eq, 1.0, 0.0)
        eq_rank = c_eq + jnp.dot(eq_f.astype(BF16), upper, preferred_element_type=F32)
        sel = (b > tau) | (eq & (eq_rank < need))
        sel_f = jnp.where(sel, 1.0, 0.0)
        rank = c_sel + jnp.dot(sel_f.astype(BF16), upper, preferred_element_type=F32)
        slot_ref[:, cols] = jnp.where(sel, rank.astype(I32), -1)
        r0_ref[...] = jnp.where(blk_lane == kb, c_sel.astype(I32), r0_ref[...])
        return (c_eq + jnp.sum(eq_f, axis=1, keepdims=True),
                c_sel + jnp.sum(sel_f, axis=1, keepdims=True))

    zero = jnp.zeros((N_EXPERTS, 1), F32)
    lax.fori_loop(0, n_blk, block, (zero, zero))


def _route(aff_t, cap):
    e, n = aff_t.shape
    n_blk = n // ROUTE_TILE
    return pl.pallas_call(
        functools.partial(_route_kernel, cap=cap),
        out_shape=[jax.ShapeDtypeStruct((e, n), I32), jax.ShapeDtypeStruct((e, n_blk), I32)],
        compiler_params=pltpu.CompilerParams(vmem_limit_bytes=VMEM_LIMIT),
        name="route",
    )(aff_t)


def _window_start(r0, cap, width):
    start = jnp.minimum((r0 // ROUTE_ALIGN) * ROUTE_ALIGN, cap - width)
    return pl.multiple_of(start, ROUTE_ALIGN)


def _tile_slots(r0_ref, e, blk, n_blk, cap):
    r0 = r0_ref[e, blk]
    nxt = r0_ref[e, jnp.minimum(blk + 1, n_blk - 1)]
    return r0, jnp.where(blk + 1 < n_blk, nxt, cap)


def _fits_small(r0_ref, experts, blk, n_blk, cap, width):
    starts, fits = [], None
    for e in experts:
        r0, r_end = _tile_slots(r0_ref, e, blk, n_blk, cap)
        start = _window_start(r0, cap, width)
        ok = r_end - start <= width
        starts.append(start)
        fits = ok if fits is None else fits & ok
    return starts, fits


def _one_hot(pick):
    return jnp.where(pick, 1.0, 0.0).astype(BF16)


def _split3(g):
    hi = g.astype(BF16).astype(F32)
    mid = (g - hi).astype(BF16).astype(F32)
    return hi, mid, (g - hi - mid).astype(BF16).astype(F32)


def _dispatch_kernel(r0_ref, x_ref, slot_ref, gate_ref, xs_ref, *, cap, n_blk):
    ep = pl.program_id(0)
    j = pl.program_id(1)
    n_sub = x_ref.shape[0] // ROUTE_TILE
    n_exp = xs_ref.shape[0]
    experts = [ep * n_exp + ee for ee in range(n_exp)]

    @pl.when(j == 0)
    def _():
        xs_ref[...] = jnp.zeros_like(xs_ref)

    def scatter(ee, tok, start, width, pick, rows):
        gate = jnp.sum(jnp.where(pick, gate_ref[ee:ee + 1, tok], 0.0), axis=1, keepdims=True)
        hi, mid, lo = _split3(gate)
        lane = lax.broadcasted_iota(I32, (width, GATE_LANES), 1)
        cols = jnp.where(lane == 0, hi, jnp.where(lane == 1, mid, jnp.where(lane == 2, lo, 0.0)))
        win = pl.ds(start, width)
        xs_ref[ee, win, :D_MODEL] = xs_ref[ee, win, :D_MODEL] + rows.astype(BF16)
        xs_ref[ee, win, D_MODEL:] = xs_ref[ee, win, D_MODEL:] + cols.astype(BF16)

    for sb in range(n_sub):
        blk = j * n_sub + sb
        tok = slice(sb * ROUTE_TILE, (sb + 1) * ROUTE_TILE)
        starts, fits = _fits_small(r0_ref, experts, blk, n_blk, cap, DISPATCH_SMALL_WIN)

        @pl.when(fits)
        def _():
            row = lax.broadcasted_iota(I32, (DISPATCH_SMALL_WIN, ROUTE_TILE), 0)
            picks = [row == slot_ref[ee:ee + 1, tok] - starts[ee] for ee in range(n_exp)]
            rows = jnp.dot(jnp.concatenate([_one_hot(p) for p in picks], axis=0), x_ref[tok, :],
                           preferred_element_type=F32)
            for ee in range(n_exp):
                scatter(ee, tok, starts[ee], DISPATCH_SMALL_WIN, picks[ee],
                        rows[ee * DISPATCH_SMALL_WIN:(ee + 1) * DISPATCH_SMALL_WIN])

        @pl.when(jnp.logical_not(fits))
        def _():
            row = lax.broadcasted_iota(I32, (ROUTE_WIN, ROUTE_TILE), 0)
            for ee in range(n_exp):
                start = _window_start(r0_ref[experts[ee], blk], cap, ROUTE_WIN)
                pick = row == slot_ref[ee:ee + 1, tok] - start
                scatter(ee, tok, start, ROUTE_WIN, pick,
                        jnp.dot(_one_hot(pick), x_ref[tok, :], preferred_element_type=F32))


def _dispatch(xb, slot_t, aff_t, r0, cap):
    n, d = xb.shape
    n_blk = n // ROUTE_TILE
    tile = min(DISPATCH_TOKENS, n)
    n_exp = min(N_EXPERTS, DISPATCH_VMEM_BYTES // (cap * (d + GATE_LANES) * 2))
    n_exp = 1 << (n_exp.bit_length() - 1)
    pairs = lambda a: a.reshape(N_EXPERTS // n_exp, n_exp, n)
    pair_spec = pl.BlockSpec((None, n_exp, tile), lambda e, j, r0: (e, 0, j))
    grid_spec = pltpu.PrefetchScalarGridSpec(
        num_scalar_prefetch=1,
        grid=(N_EXPERTS // n_exp, n // tile),
        in_specs=[pl.BlockSpec((tile, d), lambda e, j, r0: (j, 0)), pair_spec, pair_spec],
        out_specs=pl.BlockSpec((n_exp, cap, d + GATE_LANES), lambda e, j, r0: (e, 0, 0),
                               pipeline_mode=pl.Buffered(1)),
    )
    return pl.pallas_call(
        functools.partial(_dispatch_kernel, cap=cap, n_blk=n_blk),
        grid_spec=grid_spec,
        out_shape=jax.ShapeDtypeStruct((N_EXPERTS, cap, d + GATE_LANES), BF16),
        compiler_params=_cparams("parallel", "arbitrary"),
        name="moe_dispatch",
    )(r0, xb, pairs(slot_t), pairs(aff_t))


def _ffn_kernel(xs_ref, w1_ref, w3_ref, w2_ref, o_ref):
    xs = xs_ref[:, :D_MODEL]
    gate = jnp.sum(xs_ref[:, D_MODEL:].astype(F32), axis=1, keepdims=True)
    acc = jnp.zeros((xs.shape[0], D_MODEL), F32)
    for c in range(EXPERT_FF // FF_CHUNK):
        cols = slice(c * FF_CHUNK, (c + 1) * FF_CHUNK)
        h1 = jnp.dot(xs, w1_ref[:, cols].astype(BF16), preferred_element_type=F32)
        h3 = jnp.dot(xs, w3_ref[:, cols].astype(BF16), preferred_element_type=F32)
        h = (jax.nn.silu(h1) * h3).astype(BF16)
        acc = acc + jnp.dot(h, w2_ref[cols, :].astype(BF16), preferred_element_type=F32)
    o_ref[...] = (acc * gate).astype(o_ref.dtype)


def _expert_ffn(xs, w1, w3, w2, layer):
    e, cap, width = xs.shape
    d = D_MODEL
    tile = min(FFN_TILE, cap)
    return pl.pallas_call(
        _ffn_kernel,
        grid=(e, cap // tile),
        in_specs=[pl.BlockSpec((None, tile, width), lambda i, c: (i, c, 0)),
                  pl.BlockSpec((None, None, d, EXPERT_FF), lambda i, c: (layer, i, 0, 0)),
                  pl.BlockSpec((None, None, d, EXPERT_FF), lambda i, c: (layer, i, 0, 0)),
                  pl.BlockSpec((None, None, EXPERT_FF, d), lambda i, c: (layer, i, 0, 0))],
        out_specs=pl.BlockSpec((None, tile, d), lambda i, c: (i, c, 0)),
        out_shape=jax.ShapeDtypeStruct((e, cap, d), BF16),
        compiler_params=_cparams("parallel", "parallel"),
        name="moe_ffn",
    )(xs, w1, w3, w2)


def _combine_kernel(r0_ref, x_ref, slot_ref, lg_ref, lb_ref, ye_ref, o_ref,
                    win_ref, big_ref, acc_ref, sem, big_sem, *, cap):
    j = pl.program_id(0)
    n_blk = pl.num_programs(0)
    buf = j % 2
    experts = range(N_EXPERTS)

    def small_copy(start, b, e):
        return pltpu.make_async_copy(ye_ref.at[e, pl.ds(start, ROUTE_SMALL_WIN)],
                                     win_ref.at[b, pl.ds(e * ROUTE_SMALL_WIN, ROUTE_SMALL_WIN)],
                                     sem.at[b, e])

    starts, fits = _fits_small(r0_ref, experts, j, n_blk, cap, ROUTE_SMALL_WIN)
    nxt = jnp.minimum(j + 1, n_blk - 1)
    nxt_starts, nxt_fits = _fits_small(r0_ref, experts, nxt, n_blk, cap, ROUTE_SMALL_WIN)

    @pl.when((j == 0) & fits)
    def _():
        for e in experts:
            small_copy(starts[e], 0, e).start()

    @pl.when((j + 1 < n_blk) & nxt_fits)
    def _():
        for e in experts:
            small_copy(nxt_starts[e], 1 - buf, e).start()

    slot = slot_ref[...]

    def pick(e, start, width):
        lane = lax.broadcasted_iota(I32, (ROUTE_TILE, width), 1)
        return _one_hot(lane == slot[:, e:e + 1] - start)

    @pl.when(fits)
    def _():
        for e in experts:
            small_copy(starts[e], buf, e).wait()
        cols = []
        for e in experts:
            rel = slot[:, e:e + 1] - starts[e]
            cols.append(jnp.where((rel >= 0) & (rel < ROUTE_SMALL_WIN), rel + e * ROUTE_SMALL_WIN, -1))
        tiles = []
        for t in range(N_EXPERTS * ROUTE_SMALL_WIN // 128):
            lane = lax.broadcasted_iota(I32, (ROUTE_TILE, 128), 1) + t * 128
            hit = None
            for e in experts:
                if e * ROUTE_SMALL_WIN < (t + 1) * 128 and (e + 1) * ROUTE_SMALL_WIN > t * 128:
                    hit = lane == cols[e] if hit is None else hit | (lane == cols[e])
            tiles.append(_one_hot(hit))
        acc_ref[...] = jnp.dot(jnp.concatenate(tiles, axis=1), win_ref[buf], preferred_element_type=F32)

    @pl.when(jnp.logical_not(fits))
    def _():
        acc = jnp.zeros((ROUTE_TILE, D_MODEL), F32)
        for e in experts:
            start = _window_start(r0_ref[e, j], cap, ROUTE_WIN)
            copy = pltpu.make_async_copy(ye_ref.at[e, pl.ds(start, ROUTE_WIN)], big_ref, big_sem)
            copy.start()
            copy.wait()
            acc = acc + jnp.dot(pick(e, start, ROUTE_WIN), big_ref[...], preferred_element_type=F32)
        acc_ref[...] = acc

    o_ref[...] = _layer_norm(DN_ALPHA * x_ref[...] + acc_ref[...], lg_ref[...], lb_ref[...])


def _combine(x, slot, ye, r0, ln_g, ln_b, cap):
    n, d = x.shape
    n_blk = n // ROUTE_TILE
    grid_spec = pltpu.PrefetchScalarGridSpec(
        num_scalar_prefetch=1,
        grid=(n_blk,),
        in_specs=[pl.BlockSpec((ROUTE_TILE, d), lambda j, r0: (j, 0)),
                  pl.BlockSpec((ROUTE_TILE, N_EXPERTS), lambda j, r0: (j, 0)),
                  pl.BlockSpec((1, d), lambda j, r0: (0, 0)),
                  pl.BlockSpec((1, d), lambda j, r0: (0, 0)),
                  pl.BlockSpec(memory_space=pl.ANY)],
        out_specs=pl.BlockSpec((ROUTE_TILE, d), lambda j, r0: (j, 0)),
        scratch_shapes=[pltpu.VMEM((2, N_EXPERTS * ROUTE_SMALL_WIN, d), BF16),
                        pltpu.VMEM((ROUTE_WIN, d), BF16),
                        pltpu.VMEM((ROUTE_TILE, d), F32),
                        pltpu.SemaphoreType.DMA((2, N_EXPERTS)),
                        pltpu.SemaphoreType.DMA(())],
    )
    return pl.pallas_call(
        functools.partial(_combine_kernel, cap=cap),
        grid_spec=grid_spec,
        out_shape=jax.ShapeDtypeStruct((n, d), F32),
        compiler_params=_cparams("arbitrary"),
        name="moe_combine",
    )(r0, x, slot, ln_g.reshape(1, d), ln_b.reshape(1, d), ye)


def _moe_layer(x, xb, aff_t, w1, w3, w2, layer, ln_g, ln_b):
    bsz, seq, d = x.shape
    n = bsz * seq
    cap = EC_CAPACITY_FACTOR * n // N_EXPERTS
    slot_t, r0 = _route(aff_t, cap)
    xs = _dispatch(xb.reshape(n, d), slot_t, aff_t, r0, cap)
    ye = _expert_ffn(xs, w1, w3, w2, layer)
    out = _combine(x.reshape(n, d), slot_t.T, ye, r0, ln_g, ln_b, cap)
    return out.reshape(bsz, seq, d)


def _trunk(x, mem, p):
    bsz = x.shape[0]
    m = mem.shape[1]
    for i in range(DEPTH):
        if i % 2 == 0:
            x = _s5_layer(x, p["a_w_in"][i // 2], p["s5_ops"][i // 2], p["a_w_glu"][i // 2],
                          p["ln_g"][i, 0], p["ln_b"][i, 0])
            pre = None
        else:
            pre = (_na_attention(x, p["b_w_qkv"][i // 2], p["na_bias"][i // 2]), p["b_w_o"][i // 2],
                   p["ln_g"][i, 0], p["ln_b"][i, 0])
        kv = _matmul(mem.reshape(bsz * m, D_MODEL), p["m_w_kv"][i], BF16).reshape(bsz, m, 2 * D_MODEL)
        x, xb, aff_t = _xattn_router(x, kv[:, :, :D_MODEL], kv[:, :, D_MODEL:], p["m_w_q"][i], p["m_w_o"][i],
                                     p["ln_g"][i, 1], p["ln_b"][i, 1], p["e_w_router_t"][i], pre)
        x = _moe_layer(x, xb, aff_t, p["e_w1"], p["e_w3"], p["e_w2"], i,
                       p["ln_g"][i, 2], p["ln_b"][i, 2])
    return x


def kernel(x_prompt, x_sample, mem_prompt, mem_sample, a_w_in, a_lam_re, a_lam_im, a_log_dt, a_b_re, a_b_im, a_c_re, a_c_im, a_d, a_w_glu, b_w_qkv, b_rpb, b_w_o, m_w_q, m_w_kv, m_w_o, e_w_router, e_w1, e_w3, e_w2, ln_g, ln_b):
    bf = lambda w: w.astype(BF16)
    p = {
        "a_w_in": bf(a_w_in), "a_w_glu": bf(a_w_glu),
        "s5_ops": [_s5_operators(a_lam_re[j], a_lam_im[j], a_log_dt[j], a_b_re[j], a_b_im[j],
                                 a_c_re[j], a_c_im[j], a_d[j]) for j in range(a_w_in.shape[0])],
        "b_w_qkv": bf(b_w_qkv), "b_w_o": bf(b_w_o),
        "na_bias": [_na_bias_table(b_rpb[j]) for j in range(b_rpb.shape[0])],
        "m_w_q": bf(m_w_q), "m_w_kv": bf(m_w_kv), "m_w_o": bf(m_w_o),
        "e_w_router_t": bf(jnp.swapaxes(e_w_router, 1, 2)),
        "e_w1": e_w1, "e_w3": e_w3, "e_w2": e_w2,
        "ln_g": ln_g.astype(F32), "ln_b": ln_b.astype(F32),
    }
    return (_trunk(x_prompt, mem_prompt, p), _trunk(x_sample, mem_sample, p))
```
